```python
import math
import jax, jax.numpy as jnp
from jax import lax
import numpy as np

D_MODEL = 2048
BATCH = 4
SEQ = 2048
DEPTH = 4
DEC_BATCH = 8
DEC_SEQ = 8
PAST_LEN = 16384
PAGE_SIZE = 128

ATT_HEAD_DIM = 128
ATT_WIDTH = D_MODEL // 2
N_ATT_HEADS = ATT_WIDTH // ATT_HEAD_DIM
ATT_SCALE = ATT_HEAD_DIM ** -0.5
Q_BLOCK = 128
FORGET_BIAS_MIN = 2.0
FORGET_BIAS_MAX = 10.0
SSM_WIDTH = D_MODEL - ATT_WIDTH
SSM_HEADDIM = 64
N_SSM_HEADS = SSM_WIDTH // SSM_HEADDIM
N_SSM_GROUPS = 2
HEADS_PER_GROUP = N_SSM_HEADS // N_SSM_GROUPS
D_STATE = 128
CONV_W = 4
CONV_DIM = SSM_WIDTH + 2 * N_SSM_GROUPS * D_STATE
SSD_CHUNK = 128
D_FF = 4 * D_MODEL
PLE_DIM = 256
RMS_EPS = 1e-6
IN_SPLITS = (ATT_WIDTH, ATT_WIDTH, ATT_WIDTH, N_ATT_HEADS, SSM_WIDTH, CONV_DIM, N_SSM_HEADS)
IN_PROJ_DIM = sum(IN_SPLITS)
IN_OFFSETS = tuple(int(o) for o in np.cumsum(IN_SPLITS)[:-1])

kernel_name = "hymba_fox_ssd_decoder_step"


def rmsnorm(x, g):
    xf = x.astype(jnp.float32)
    y = xf * lax.rsqrt(jnp.mean(xf * xf, axis=-1, keepdims=True) + RMS_EPS)
    return (y * g.astype(jnp.float32)).astype(x.dtype)


def fox_attend(q, k, v, q_cum, k_cum, q_pos, k_pos):
    s = jnp.einsum('bqhd,bkhd->bhqk', q, k).astype(jnp.float32) * ATT_SCALE
    s = s + (jnp.swapaxes(q_cum, 1, 2)[:, :, :, None] - jnp.swapaxes(k_cum, 1, 2)[:, :, None, :])
    s = jnp.where((k_pos[None, :] <= q_pos[:, None])[None, None], s, -jnp.inf)
    p = jax.nn.softmax(s, axis=-1)
    return jnp.einsum('bhqk,bkhd->bqhd', p.astype(v.dtype), v)


def fox_prompt(q, k, v, logf):
    bsz, S, H, Dh = q.shape
    nb = S // Q_BLOCK
    cum = jnp.cumsum(logf, axis=1)
    qb = jnp.moveaxis(q.reshape(bsz, nb, Q_BLOCK, H, Dh), 1, 0)
    cb = jnp.moveaxis(cum.reshape(bsz, nb, Q_BLOCK, H), 1, 0)
    starts = jnp.arange(nb, dtype=jnp.int32) * Q_BLOCK
    k_pos = jnp.arange(S, dtype=jnp.int32)

    def one_block(args):
        q_blk, c_blk, s0 = args
        q_pos = s0 + jnp.arange(Q_BLOCK, dtype=jnp.int32)
        return fox_attend(q_blk, k, v, c_blk, cum, q_pos, k_pos)

    out = lax.map(one_block, (qb, cb, starts))
    return jnp.moveaxis(out, 0, 1).reshape(bsz, S, H, Dh)


def fox_sample(q, k_new, v_new, logf_new, k_past, v_past, logf_past):
    T = q.shape[1]
    P = k_past.shape[1]
    logf_past = logf_past.astype(jnp.float32)
    q_cum = jnp.cumsum(logf_new, axis=1)
    past_excl = lax.cumsum(logf_past, axis=1, reverse=True) - logf_past
    k_cum = jnp.concatenate([-past_excl, q_cum], axis=1)
    k_all = jnp.concatenate([k_past, k_new], axis=1)
    v_all = jnp.concatenate([v_past, v_new], axis=1)
    q_pos = P + jnp.arange(T, dtype=jnp.int32)
    k_pos = jnp.arange(P + T, dtype=jnp.int32)
    return fox_attend(q, k_all, v_all, q_cum, k_cum, q_pos, k_pos)


def causal_conv(xbc, conv_prev, w, b):
    L = xbc.shape[1]
    full = jnp.concatenate([conv_prev, xbc], axis=1)
    out = full[:, 0:L] * w[0]
    for j in range(1, CONV_W):
        out = out + full[:, j:j + L] * w[j]
    return jax.nn.silu(out + b), full[:, full.shape[1] - (CONV_W - 1):]


def ssd_chunked(x, dt, a_neg, b_in, c_in, h0, chunk):
    bsz, L, H, P = x.shape
    nc = L // chunk
    x = x.reshape(bsz, nc, chunk, H, P)
    dt = dt.reshape(bsz, nc, chunk, H)
    bh = jnp.repeat(b_in.reshape(bsz, nc, chunk, N_SSM_GROUPS, D_STATE), HEADS_PER_GROUP, axis=3)
    ch = jnp.repeat(c_in.reshape(bsz, nc, chunk, N_SSM_GROUPS, D_STATE), HEADS_PER_GROUP, axis=3)
    acum = jnp.cumsum(dt * a_neg, axis=2)
    seg = acum[:, :, :, None, :] - acum[:, :, None, :, :]
    causal = jnp.tril(jnp.ones((chunk, chunk), dtype=bool))[None, None, :, :, None]
    decay = jnp.exp(jnp.where(causal, seg, -jnp.inf))
    cb = jnp.einsum('bcthn,bcshn->bctsh', ch, bh)
    y_diag = jnp.einsum('bctsh,bcsh,bcshp->bcthp', cb * decay, dt, x)
    to_end = jnp.exp(acum[:, :, -1:, :] - acum)
    states = jnp.einsum('bcsh,bcshn,bcshp->bchpn', to_end * dt, bh, x)
    chunk_decay = jnp.exp(acum[:, :, -1, :])

    def step(h, inp):
        st, dec = inp
        return dec[:, :, None, None] * h + st, h

    h_final, h_prev = lax.scan(step, h0, (jnp.moveaxis(states, 1, 0), jnp.moveaxis(chunk_decay, 1, 0)))
    h_prev = jnp.moveaxis(h_prev, 0, 1)
    y_off = jnp.einsum('bcthn,bchpn,bcth->bcthp', ch, h_prev, jnp.exp(acum))
    return (y_diag + y_off).reshape(bsz, L, H, P), h_final


def ssm_branch(z, xbc, dt_raw, conv_prev, h0, conv_w, conv_b, dt_bias, a_log, d_skip, g_out, chunk):
    bsz, L, _ = z.shape
    xbc_c, new_conv = causal_conv(xbc, conv_prev, conv_w, conv_b)
    xs, bm, cm = jnp.split(xbc_c, [SSM_WIDTH, SSM_WIDTH + N_SSM_GROUPS * D_STATE], axis=-1)
    xs = xs.reshape(bsz, L, N_SSM_HEADS, SSM_HEADDIM).astype(jnp.float32)
    bm = bm.reshape(bsz, L, N_SSM_GROUPS, D_STATE).astype(jnp.float32)
    cm = cm.reshape(bsz, L, N_SSM_GROUPS, D_STATE).astype(jnp.float32)
    dt = jax.nn.softplus(dt_raw.astype(jnp.float32) + dt_bias.astype(jnp.float32))
    a_neg = -jnp.exp(a_log.astype(jnp.float32))
    y, h_final = ssd_chunked(xs, dt, a_neg, bm, cm, h0.astype(jnp.float32), chunk)
    y = y + d_skip.astype(jnp.float32)[:, None] * xs
    y = y.reshape(bsz, L, SSM_WIDTH) * jax.nn.silu(z.astype(jnp.float32))
    return rmsnorm(y, g_out).astype(z.dtype), new_conv, h_final


def setup_inputs(seed: int = 0) -> dict:
    key = jax.random.key(seed)
    ks = jax.random.split(key, 32)
    f32 = jnp.float32
    n_pages = PAST_LEN // PAGE_SIZE
    n_phys = (DEC_BATCH * n_pages * 5) // 4

    def nrm(k, shape, scale):
        return jax.random.normal(k, shape, f32) * scale

    def gain(k, shape):
        return 1.0 + 0.02 * jax.random.normal(k, shape, f32)

    page_table = jax.random.permutation(ks[7], n_phys)[: DEC_BATCH * n_pages]
    page_table = page_table.reshape(DEC_BATCH, n_pages).astype(jnp.int32)
    dt0 = jnp.exp(jax.random.uniform(ks[13], (DEPTH, N_SSM_HEADS), f32) * (math.log(0.1) - math.log(0.001)) + math.log(0.001))
    forget_centres = jnp.linspace(FORGET_BIAS_MIN, FORGET_BIAS_MAX, N_ATT_HEADS, dtype=f32)
    return {
        'x_prompt': nrm(ks[0], (BATCH, SEQ, D_MODEL), 1.0),
        'x_sample': nrm(ks[1], (DEC_BATCH, DEC_SEQ, D_MODEL), 1.0),
        'cache_k': nrm(ks[2], (DEPTH, n_phys, PAGE_SIZE, N_ATT_HEADS, ATT_HEAD_DIM), 1.0),
        'cache_v': nrm(ks[3], (DEPTH, n_phys, PAGE_SIZE, N_ATT_HEADS, ATT_HEAD_DIM), 1.0),
        'cache_logf': jax.nn.log_sigmoid(forget_centres + nrm(ks[4], (DEPTH, n_phys, PAGE_SIZE, N_ATT_HEADS), 1.0)),
        'state_ssm': nrm(ks[5], (DEPTH, DEC_BATCH, N_SSM_HEADS, SSM_HEADDIM, D_STATE), 0.1),
        'state_conv': nrm(ks[6], (DEPTH, DEC_BATCH, CONV_W - 1, CONV_DIM), 1.0),
        'page_table': page_table,
        'p_prompt': nrm(ks[8], (DEPTH, BATCH, SEQ, PLE_DIM), 1.0),
        'p_sample': nrm(ks[9], (DEPTH, DEC_BATCH, DEC_SEQ, PLE_DIM), 1.0),
        'norm_mix': gain(ks[10], (DEPTH, D_MODEL)),
        'w_in': nrm(ks[11], (DEPTH, D_MODEL, IN_PROJ_DIM), D_MODEL ** -0.5),
        'b_forget': forget_centres + nrm(ks[12], (DEPTH, N_ATT_HEADS), 0.1),
        'conv_w': nrm(ks[14], (DEPTH, CONV_W, CONV_DIM), CONV_W ** -0.5),
        'conv_b': nrm(ks[15], (DEPTH, CONV_DIM), 0.01),
        'dt_bias': dt0 + jnp.log(-jnp.expm1(-dt0)),
        'a_log': jnp.log(jax.random.uniform(ks[16], (DEPTH, N_SSM_HEADS), f32, 1.0, 16.0)),
        'd_skip': gain(ks[17], (DEPTH, N_SSM_HEADS)),
        'norm_attn_out': gain(ks[18], (DEPTH, ATT_WIDTH)),
        'norm_ssm_out': gain(ks[19], (DEPTH, SSM_WIDTH)),
        'w_out': nrm(ks[20], (DEPTH, ATT_WIDTH + SSM_WIDTH, D_MODEL), (ATT_WIDTH + SSM_WIDTH) ** -0.5),
        'norm_mlp': gain(ks[21], (DEPTH, D_MODEL)),
        'w_up': nrm(ks[22], (DEPTH, D_MODEL, D_FF), D_MODEL ** -0.5),
        'w_down': nrm(ks[23], (DEPTH, D_FF, D_MODEL), D_FF ** -0.5),
        'norm_ple': gain(ks[24], (DEPTH, D_MODEL)),
        'w_ple_gate': nrm(ks[25], (DEPTH, D_MODEL, D_MODEL), D_MODEL ** -0.5),
        'w_ple_proj': nrm(ks[26], (DEPTH, PLE_DIM, D_MODEL), PLE_DIM ** -0.5),
        'norm_final': gain(ks[27], (D_MODEL,)),
    }


def reference(x_prompt, x_sample, cache_k, cache_v, cache_logf, state_ssm, state_conv, page_table,
              p_prompt, p_sample, norm_mix, w_in, b_forget, conv_w, conv_b, dt_bias, a_log, d_skip,
              norm_attn_out, norm_ssm_out, w_out, norm_mlp, w_up, w_down, norm_ple, w_ple_gate,
              w_ple_proj, norm_final):
    n_seq, n_pages = page_table.shape
    past_len = n_pages * cache_k.shape[2]

    def layer(h, p_l, l, conv_prev, h0, attend, chunk):
        bsz, L, _ = h.shape
        hn = rmsnorm(h, norm_mix[l])
        q, k, v, f_raw, z, xbc, dt_raw = jnp.split(hn @ w_in[l], IN_OFFSETS, axis=-1)
        q = q.reshape(bsz, L, N_ATT_HEADS, ATT_HEAD_DIM)
        k = k.reshape(bsz, L, N_ATT_HEADS, ATT_HEAD_DIM)
        v = v.reshape(bsz, L, N_ATT_HEADS, ATT_HEAD_DIM)
        logf = jax.nn.log_sigmoid(f_raw.astype(jnp.float32) + b_forget[l].astype(jnp.float32))
        o_att = rmsnorm(attend(q, k, v, logf).reshape(bsz, L, ATT_WIDTH), norm_attn_out[l])
        o_ssm, new_conv, h_fin = ssm_branch(z, xbc, dt_raw, conv_prev, h0, conv_w[l], conv_b[l],
                                            dt_bias[l], a_log[l], d_skip[l], norm_ssm_out[l], chunk)
        h = h + jnp.concatenate([o_att, o_ssm], axis=-1) @ w_out[l]
        u = jax.nn.relu(rmsnorm(h, norm_mlp[l]) @ w_up[l])
        h = h + (u * u) @ w_down[l]
        gate = jax.nn.sigmoid(rmsnorm(h, norm_ple[l]) @ w_ple_gate[l])
        h = h + (p_l @ w_ple_proj[l]) * gate
        return h, k, v, logf, new_conv, h_fin

    hp, hs = x_prompt, x_sample
    kp, vp, fp, sp, cp = [], [], [], [], []
    ksl, vsl, fsl, ssl, csl = [], [], [], [], []
    for l in range(DEPTH):
        conv0 = jnp.zeros((hp.shape[0], CONV_W - 1, CONV_DIM), hp.dtype)
        h00 = jnp.zeros((hp.shape[0], N_SSM_HEADS, SSM_HEADDIM, D_STATE), jnp.float32)
        hp, k_l, v_l, f_l, c_l, s_l = layer(hp, p_prompt[l], l, conv0, h00, fox_prompt, SSD_CHUNK)
        kp.append(k_l); vp.append(v_l); fp.append(f_l); cp.append(c_l); sp.append(s_l)

        k_past = cache_k[l][page_table].reshape(n_seq, past_len, N_ATT_HEADS, ATT_HEAD_DIM)
        v_past = cache_v[l][page_table].reshape(n_seq, past_len, N_ATT_HEADS, ATT_HEAD_DIM)
        f_past = cache_logf[l][page_table].reshape(n_seq, past_len, N_ATT_HEADS)

        def attend_sample(q, k, v, logf, k_past=k_past, v_past=v_past, f_past=f_past):
            return fox_sample(q, k, v, logf, k_past, v_past, f_past)

        hs, k_l, v_l, f_l, c_l, s_l = layer(hs, p_sample[l], l, state_conv[l], state_ssm[l],
                                            attend_sample, hs.shape[1])
        ksl.append(k_l); vsl.append(v_l); fsl.append(f_l); csl.append(c_l); ssl.append(s_l)

    y_prompt = rmsnorm(hp, norm_final)
    y_sample = rmsnorm(hs, norm_final)
    return (y_prompt, y_sample,
            jnp.stack(kp), jnp.stack(vp), jnp.stack(fp), jnp.stack(sp), jnp.stack(cp),
            jnp.stack(ksl), jnp.stack(vsl), jnp.stack(fsl), jnp.stack(ssl), jnp.stack(csl))
```

```python
import functools

import numpy as np
import jax
import jax.numpy as jnp
from jax import lax
from jax.experimental import pallas as pl
from jax.experimental.pallas import tpu as pltpu

F32 = jnp.float32
BF16 = jnp.bfloat16
HIGHEST = lax.Precision.HIGHEST

D_MODEL = 2048
ATT_HEAD_DIM = 128
ATT_WIDTH = D_MODEL // 2
N_ATT_HEADS = ATT_WIDTH // ATT_HEAD_DIM
ATT_SCALE = ATT_HEAD_DIM ** -0.5
SSM_WIDTH = D_MODEL - ATT_WIDTH
SSM_HEADDIM = 64
N_SSM_HEADS = SSM_WIDTH // SSM_HEADDIM
N_SSM_GROUPS = 2
GROUP_WIDTH = SSM_WIDTH // N_SSM_GROUPS
D_STATE = 128
CONV_W = 4
BC_WIDTH = 2 * N_SSM_GROUPS * D_STATE
CONV_DIM = SSM_WIDTH + BC_WIDTH
D_FF = 4 * D_MODEL
RMS_EPS = 1e-6
NEG = -1e30

SUBLANES = 8
LANES = 128
VMEM_LIMIT = 56 * 1024 * 1024

COL_Q = 0
COL_K = ATT_WIDTH
COL_V = 2 * ATT_WIDTH
COL_Z = 3 * ATT_WIDTH
COL_XS = COL_Z + SSM_WIDTH
COL_BC = COL_XS + SSM_WIDTH
COL_MISC = COL_BC + BC_WIDTH
PROJ_W = COL_MISC + LANES
F_LANE = 0
DT_LANE = N_ATT_HEADS

CHUNK = 128
TOKEN_TILE = 768
ROW_ALIGN = 256


def _rms(x, g):
    ms = jnp.mean(x * x, axis=-1, keepdims=True)
    return (x * lax.rsqrt(ms + RMS_EPS)) * g


def _sigmoid(x):
    return 1.0 / (1.0 + jnp.exp(-x))


def _softplus(x):
    return jnp.maximum(x, 0.0) + jnp.log1p(jnp.exp(-jnp.abs(x)))


def _dot(a, b):
    return jnp.dot(a, b, preferred_element_type=F32)


def _dot_exact(a, b):
    return jnp.dot(a, b, precision=HIGHEST, preferred_element_type=F32)


def _dot_nt(a, b):
    return lax.dot_general(a, b, (((1,), (1,)), ((), ())), preferred_element_type=F32)


def _params(*semantics):
    return pltpu.CompilerParams(dimension_semantics=semantics, vmem_limit_bytes=VMEM_LIMIT)


def _in_proj_kernel(x_ref, g_ref, w_ref, o_ref, xn_ref):
    @pl.when(pl.program_id(1) == 0)
    def _():
        xn_ref[...] = _rms(x_ref[...], g_ref[...]).astype(BF16)

    o_ref[...] = _dot(xn_ref[...], w_ref[...])


def _in_proj(h, g, w):
    t = h.shape[0]
    tm = TOKEN_TILE
    tn = PROJ_W // 3
    return pl.pallas_call(
        _in_proj_kernel,
        grid=(t // tm, PROJ_W // tn),
        in_specs=[
            pl.BlockSpec((tm, D_MODEL), lambda i, j: (i, 0)),
            pl.BlockSpec((1, D_MODEL), lambda i, j: (0, 0)),
            pl.BlockSpec((D_MODEL, tn), lambda i, j: (0, j)),
        ],
        out_specs=pl.BlockSpec((tm, tn), lambda i, j: (i, j)),
        out_shape=jax.ShapeDtypeStruct((t, PROJ_W), F32),
        scratch_shapes=[pltpu.VMEM((tm, D_MODEL), BF16)],
        compiler_params=_params("parallel", "arbitrary"),
        name="in_proj",
    )(h, g, w)


def _gates_kernel(misc_ref, fb_ref, tri_ref, logf_ref, cum_ref, cumt_ref, carry_ref, *,
                  chunks_per_seq, n_prompt_chunks):
    i = pl.program_id(0)

    @pl.when(jnp.logical_or(lax.rem(i, chunks_per_seq) == 0, i >= n_prompt_chunks))
    def _():
        carry_ref[...] = jnp.zeros_like(carry_ref)

    x = misc_ref[...] + fb_ref[...]
    lf = jnp.minimum(x, 0.0) - jnp.log1p(jnp.exp(-jnp.abs(x)))
    c = _dot_exact(tri_ref[...], lf) + carry_ref[...]
    carry_ref[...] = c[CHUNK - 1:CHUNK, :]
    logf_ref[...] = lf[:, F_LANE:F_LANE + N_ATT_HEADS]
    cum_ref[...] = c[:, F_LANE:F_LANE + N_ATT_HEADS]
    cumt_ref[...] = c.T[F_LANE:F_LANE + N_ATT_HEADS, :]


def _gates(proj, fb_row, tri2, n_prompt_rows, seq):
    n_prompt_chunks = n_prompt_rows // CHUNK
    n_chunks = n_prompt_chunks + 1
    rows = n_chunks * CHUNK
    kern = functools.partial(_gates_kernel, chunks_per_seq=seq // CHUNK,
                             n_prompt_chunks=n_prompt_chunks)
    return pl.pallas_call(
        kern,
        grid=(n_chunks,),
        in_specs=[
            pl.BlockSpec((CHUNK, LANES), lambda i: (i, COL_MISC // LANES)),
            pl.BlockSpec((1, LANES), lambda i: (0, 0)),
            pl.BlockSpec((None, CHUNK, CHUNK), lambda i: (i // n_prompt_chunks, 0, 0)),
        ],
        out_specs=[
            pl.BlockSpec((CHUNK, N_ATT_HEADS), lambda i: (i, 0)),
            pl.BlockSpec((CHUNK, N_ATT_HEADS), lambda i: (i, 0)),
            pl.BlockSpec((N_ATT_HEADS, CHUNK), lambda i: (0, i)),
        ],
        out_shape=[
            jax.ShapeDtypeStruct((rows, N_ATT_HEADS), F32),
            jax.ShapeDtypeStruct((rows, N_ATT_HEADS), F32),
            jax.ShapeDtypeStruct((N_ATT_HEADS, rows), F32),
        ],
        scratch_shapes=[pltpu.VMEM((1, LANES), F32)],
        compiler_params=_params("arbitrary"),
        name="gates",
    )(proj, fb_row, tri2)


def _fox_prompt_kernel(qi_ref, ki_ref, q_ref, k_ref, v_ref, cq_ref, ck_ref, g_ref, o_ref,
                       m_ref, l_ref, acc_ref, *, tq, tk):
    p = pl.program_id(1)
    qi = qi_ref[p]
    ki = ki_ref[p]

    @pl.when(ki == 0)
    def _():
        m_ref[...] = jnp.full_like(m_ref, NEG)
        l_ref[...] = jnp.zeros_like(l_ref)
        acc_ref[...] = jnp.zeros_like(acc_ref)

    row = lax.broadcasted_iota(jnp.int32, (tq, tk), 0) + qi * tq
    col = lax.broadcasted_iota(jnp.int32, (tq, tk), 1) + ki * tk
    visible = col <= row
    cq = cq_ref[...]
    ck = ck_ref[...]
    for h in range(N_ATT_HEADS):
        sl = slice(h * ATT_HEAD_DIM, (h + 1) * ATT_HEAD_DIM)
        s = _dot_nt(q_ref[:, sl].astype(BF16), k_ref[:, sl].astype(BF16))
        s = s * ATT_SCALE + (cq[:, h:h + 1] - ck[h:h + 1, :])
        s = jnp.where(visible, s, NEG)
        m_prev = m_ref[h]
        m_new = jnp.maximum(m_prev, jnp.max(s, axis=-1, keepdims=True))
        alpha = jnp.exp(m_prev - m_new)
        pm = jnp.exp(s - m_new)
        l_ref[h] = alpha * l_ref[h] + jnp.sum(pm, axis=-1, keepdims=True)
        acc_ref[:, sl] = alpha * acc_ref[:, sl] + _dot(pm.astype(BF16), v_ref[:, sl].astype(BF16))
        m_ref[h] = m_new

    @pl.when(ki == qi)
    def _():
        for h in range(N_ATT_HEADS):
            sl = slice(h * ATT_HEAD_DIM, (h + 1) * ATT_HEAD_DIM)
            acc_ref[:, sl] = acc_ref[:, sl] / l_ref[h]
        o_ref[...] = _rms(acc_ref[...], g_ref[...])


def _fox_prompt(proj, cum, cumt, g, batch, seq):
    tq = tk = 256
    nq = seq // tq
    pairs = [(a, b) for a in range(nq) for b in range(a + 1)]
    qi_tab = jnp.asarray(np.array([a for a, _ in pairs], np.int32))
    ki_tab = jnp.asarray(np.array([b for _, b in pairs], np.int32))
    kern = functools.partial(_fox_prompt_kernel, tq=tq, tk=tk)
    grid_spec = pltpu.PrefetchScalarGridSpec(
        num_scalar_prefetch=2,
        grid=(batch, len(pairs)),
        in_specs=[
            pl.BlockSpec((tq, ATT_WIDTH), lambda b, p, qt, kt: (b * nq + qt[p], COL_Q // ATT_WIDTH)),
            pl.BlockSpec((tk, ATT_WIDTH), lambda b, p, qt, kt: (b * nq + kt[p], COL_K // ATT_WIDTH)),
            pl.BlockSpec((tk, ATT_WIDTH), lambda b, p, qt, kt: (b * nq + kt[p], COL_V // ATT_WIDTH)),
            pl.BlockSpec((tq, N_ATT_HEADS), lambda b, p, qt, kt: (b * nq + qt[p], 0)),
            pl.BlockSpec((N_ATT_HEADS, tk), lambda b, p, qt, kt: (0, b * nq + kt[p])),
            pl.BlockSpec((1, ATT_WIDTH), lambda b, p, qt, kt: (0, 0)),
        ],
        out_specs=pl.BlockSpec((tq, ATT_WIDTH), lambda b, p, qt, kt: (b * nq + qt[p], 0)),
        scratch_shapes=[
            pltpu.VMEM((N_ATT_HEADS, tq, 1), F32),
            pltpu.VMEM((N_ATT_HEADS, tq, 1), F32),
            pltpu.VMEM((tq, ATT_WIDTH), F32),
        ],
    )
    return pl.pallas_call(
        kern,
        grid_spec=grid_spec,
        out_shape=jax.ShapeDtypeStruct((batch * seq, ATT_WIDTH), F32),
        compiler_params=_params("parallel", "arbitrary"),
        name="fox_prompt",
    )(qi_tab, ki_tab, proj, proj, proj, cum, cumt, g)


def _past_bias_kernel(pt_ref, logf_hbm, m1_ref, s_ref, su_ref, e8_ref, o_ref, x_ref, sem, *,
                      n_pages):
    l = pl.program_id(0)
    b = pl.program_id(1)

    def page_copy(p):
        page = pt_ref[b * n_pages + p]
        return pltpu.make_async_copy(logf_hbm.at[l, pl.ds(page, 1), :], x_ref.at[pl.ds(p, 1), :], sem)

    def start(p, carry):
        page_copy(p).start()
        return carry

    def wait(p, carry):
        page_copy(p).wait()
        return carry

    lax.fori_loop(0, n_pages, start, 0)
    lax.fori_loop(0, n_pages, wait, 0)
    x = x_ref[...]
    within = _dot_exact(x, m1_ref[...])
    totals = _dot_exact(x, s_ref[...])
    later = _dot_exact(su_ref[...], totals)
    o_ref[...] = within + _dot_exact(later, e8_ref[...])


def _past_bias(cache_logf, page_table):
    depth, n_phys, page, heads = cache_logf.shape
    n_seq, n_pages = page_table.shape
    width = page * heads
    r = lax.broadcasted_iota(jnp.int32, (width, width), 0)
    c = lax.broadcasted_iota(jnp.int32, (width, width), 1)
    m1 = ((r % heads == c // page) & (r // heads > c % page)).astype(F32)
    r = lax.broadcasted_iota(jnp.int32, (width, LANES), 0)
    c = lax.broadcasted_iota(jnp.int32, (width, LANES), 1)
    s = (r % heads == c).astype(F32)
    r = lax.broadcasted_iota(jnp.int32, (n_pages, n_pages), 0)
    c = lax.broadcasted_iota(jnp.int32, (n_pages, n_pages), 1)
    su = (c > r).astype(F32)
    r = lax.broadcasted_iota(jnp.int32, (LANES, width), 0)
    c = lax.broadcasted_iota(jnp.int32, (LANES, width), 1)
    e8 = (r == c // page).astype(F32)
    const = lambda shape: pl.BlockSpec(shape, lambda l, b, pt: (0, 0))
    grid_spec = pltpu.PrefetchScalarGridSpec(
        num_scalar_prefetch=1,
        grid=(depth, n_seq),
        in_specs=[
            pl.BlockSpec(memory_space=pl.ANY),
            const((width, width)), const((width, LANES)), const((n_pages, n_pages)),
            const((LANES, width)),
        ],
        out_specs=pl.BlockSpec((None, None, n_pages, width), lambda l, b, pt: (l, b, 0, 0)),
        scratch_shapes=[pltpu.VMEM((n_pages, width), F32), pltpu.SemaphoreType.DMA(())],
    )
    out = pl.pallas_call(
        functools.partial(_past_bias_kernel, n_pages=n_pages),
        grid_spec=grid_spec,
        out_shape=jax.ShapeDtypeStruct((depth, n_seq, n_pages, width), F32),
        compiler_params=_params("arbitrary", "arbitrary"),
        name="past_bias",
    )(page_table.reshape(-1), cache_logf.reshape(depth, n_phys, width), m1, s, su, e8)
    return out.reshape(depth, n_seq, n_pages, heads, page)


def _fox_sample_kernel(pt_ref, q_ref, kn_ref, vn_ref, qc_ref, bn_ref, pb_ref, g_ref, *rest,
                       pages_per_step, t_new):
    k_refs = rest[:pages_per_step]
    v_refs = rest[pages_per_step:2 * pages_per_step]
    o_ref, qall_ref, kpad_ref, vpad_ref, m_ref, l_ref, acc_ref = rest[2 * pages_per_step:]
    p = pl.program_id(1)

    def head_lanes(h):
        return slice(h * ATT_HEAD_DIM, (h + 1) * ATT_HEAD_DIM)

    def head_rows(h):
        return slice(h * t_new, (h + 1) * t_new)

    @pl.when(p == 0)
    def _():
        m_ref[...] = jnp.full_like(m_ref, NEG)
        l_ref[...] = jnp.zeros_like(l_ref)
        acc_ref[...] = jnp.zeros_like(acc_ref)
        for h in range(N_ATT_HEADS):
            qall_ref[head_rows(h), :] = q_ref[:, head_lanes(h)]

    qc = qc_ref[...]
    qall = qall_ref[...].astype(BF16)

    def attend(k_head, v_head, bias):
        s = jnp.concatenate(
            [_dot_nt(qall, k_head(h).astype(BF16))[head_rows(h), :] for h in range(N_ATT_HEADS)], axis=0)
        s = s * ATT_SCALE + (qc + bias)
        m_prev = m_ref[...]
        m_new = jnp.maximum(m_prev, jnp.max(s, axis=-1, keepdims=True))
        alpha = jnp.exp(m_prev - m_new)
        pm = jnp.exp(s - m_new)
        l_ref[...] = alpha * l_ref[...] + jnp.sum(pm, axis=-1, keepdims=True)
        pm = pm.astype(BF16)
        for h in range(N_ATT_HEADS):
            acc_ref[:, head_lanes(h)] = (alpha * acc_ref[:, head_lanes(h)]
                                         + _dot(pm, v_head(h).astype(BF16)))
        m_ref[...] = m_new

    for i in range(pages_per_step):
        pb = pb_ref[i]
        bias = jnp.concatenate(
            [jnp.broadcast_to(pb[h:h + 1, :], (t_new, pb.shape[1])) for h in range(N_ATT_HEADS)], axis=0)
        attend(lambda h, r=k_refs[i]: r[:, h, :], lambda h, r=v_refs[i]: r[:, h, :], bias)

    @pl.when(p == pl.num_programs(1) - 1)
    def _():
        kpad_ref[...] = jnp.zeros_like(kpad_ref)
        vpad_ref[...] = jnp.zeros_like(vpad_ref)
        kpad_ref[0:t_new, :] = kn_ref[...]
        vpad_ref[0:t_new, :] = vn_ref[...]
        attend(lambda h: kpad_ref[:, head_lanes(h)], lambda h: vpad_ref[:, head_lanes(h)], bn_ref[...])
        o_full = acc_ref[...] / l_ref[...]
        o = jnp.concatenate(
            [o_full[h * t_new:(h + 1) * t_new, h * ATT_HEAD_DIM:(h + 1) * ATT_HEAD_DIM]
             for h in range(N_ATT_HEADS)], axis=1)
        o_ref[...] = _rms(o, g_ref[...])


def _fox_sample(proj, qc_col, bias_new, pbias, cache_k, cache_v, page_table, g, layer, row0, t_new):
    n_seq, n_pages = page_table.shape
    page = cache_k.shape[2]
    pages_per_step = 8
    steps = n_pages // pages_per_step
    rows = N_ATT_HEADS * t_new
    rb0 = row0 // t_new

    def page_spec(i):
        return pl.BlockSpec(
            (None, None, page, N_ATT_HEADS, ATT_HEAD_DIM),
            lambda b, p, pt: (layer, pt[b * n_pages + p * pages_per_step + i], 0, 0, 0))

    grid_spec = pltpu.PrefetchScalarGridSpec(
        num_scalar_prefetch=1,
        grid=(n_seq, steps),
        in_specs=[
            pl.BlockSpec((t_new, ATT_WIDTH), lambda b, p, pt: (rb0 + b, COL_Q // ATT_WIDTH)),
            pl.BlockSpec((t_new, ATT_WIDTH), lambda b, p, pt: (rb0 + b, COL_K // ATT_WIDTH)),
            pl.BlockSpec((t_new, ATT_WIDTH), lambda b, p, pt: (rb0 + b, COL_V // ATT_WIDTH)),
            pl.BlockSpec((None, rows, 1), lambda b, p, pt: (b, 0, 0)),
            pl.BlockSpec((None, rows, page), lambda b, p, pt: (b, 0, 0)),
            pl.BlockSpec((None, None, pages_per_step, N_ATT_HEADS, page),
                         lambda b, p, pt: (layer, b, p, 0, 0)),
            pl.BlockSpec((1, ATT_WIDTH), lambda b, p, pt: (0, 0)),
        ] + [page_spec(i) for i in range(pages_per_step)] * 2,
        out_specs=pl.BlockSpec((t_new, ATT_WIDTH), lambda b, p, pt: (b, 0)),
        scratch_shapes=[
            pltpu.VMEM((rows, ATT_HEAD_DIM), F32),
            pltpu.VMEM((page, ATT_WIDTH), F32),
            pltpu.VMEM((page, ATT_WIDTH), F32),
            pltpu.VMEM((rows, 1), F32),
            pltpu.VMEM((rows, 1), F32),
            pltpu.VMEM((rows, ATT_WIDTH), F32),
        ],
    )
    kern = functools.partial(_fox_sample_kernel, pages_per_step=pages_per_step, t_new=t_new)
    return pl.pallas_call(
        kern,
        grid_spec=grid_spec,
        out_shape=jax.ShapeDtypeStruct((n_seq * t_new, ATT_WIDTH), F32),
        compiler_params=_params("parallel", "arbitrary"),
        name="fox_sample",
    )(page_table.reshape(-1), proj, proj, proj, qc_col, bias_new, pbias, g,
      *([cache_k] * pages_per_step), *([cache_v] * pages_per_step))


def _causal_conv(x, prev, w, b, rows_per_seq):
    n = x.shape[0]
    row = lax.broadcasted_iota(jnp.int32, x.shape, 0)
    out = None
    for j in range(CONV_W - 1, 0, -1):
        if rows_per_seq is None:
            shifted = jnp.where(row >= j, pltpu.roll(x, j, 0), pltpu.roll(prev, j, 0))
        else:
            shifted = jnp.where(lax.rem(row, rows_per_seq) >= j, pltpu.roll(x, j, 0),
                                pltpu.roll(prev, (j - rows_per_seq) % n, 0))
        tap = CONV_W - 1 - j
        term = shifted * w[tap:tap + 1, :]
        out = term if out is None else out + term
    out = out + x * w[CONV_W - 1:CONV_W, :]
    out = out + b
    return out * _sigmoid(out)


def _ssd_scalars(misc, dtb, alog, valid, tri, lsel):
    dtf = _softplus(misc + dtb)
    a = dtf * (-(jnp.exp(alog) * valid))
    acum = _dot_exact(tri, a)
    acum_end = _dot_exact(lsel, acum)
    return dtf, acum, acum_end


def _ssd_diag(xs, bm, cm, dtf, acum, visible):
    acum_t = acum.T
    dt_t = dtf.T
    lane = lax.broadcasted_iota(jnp.int32, (xs.shape[0], LANES), 1)
    heads_per_group = N_SSM_HEADS // N_SSM_GROUPS
    blocks = []
    for g in range(N_SSM_GROUPS):
        gs = slice(g * D_STATE, (g + 1) * D_STATE)
        cb = _dot_nt(cm[:, gs].astype(BF16), bm[:, gs].astype(BF16))
        for j in range(heads_per_group // 2):
            blk = g * (heads_per_group // 2) + j
            xpair = xs[:, blk * LANES:(blk + 1) * LANES]
            y = None
            for hh in range(2):
                ch = DT_LANE + 2 * blk + hh
                seg = acum[:, ch:ch + 1] - acum_t[ch:ch + 1, :]
                decay = jnp.exp(jnp.where(visible, seg, NEG))
                m = (cb * decay) * dt_t[ch:ch + 1, :]
                in_half = (lane >= SSM_HEADDIM) if hh else (lane < SSM_HEADDIM)
                xh = jnp.where(in_half, xpair, 0.0)
                part = _dot(m.astype(BF16), xh.astype(BF16))
                y = part if y is None else y + part
            blocks.append(y)
    return jnp.concatenate(blocks, axis=1)


def _ssd_prompt_kernel(z_ref, xs_ref, bc_ref, misc_ref, cw_ref, cb_ref, dtb_ref, alog_ref,
                       valid_ref, drow_ref, g_ref, tri_ref, lsel_ref, e_ref,
                       o_ref, st_ref, ht_ref, pxs_ref, pbc_ref, *, chunks_per_seq):
    c = lax.rem(pl.program_id(0), chunks_per_seq)

    @pl.when(c == 0)
    def _():
        ht_ref[...] = jnp.zeros_like(ht_ref)
        pxs_ref[...] = jnp.zeros_like(pxs_ref)
        pbc_ref[...] = jnp.zeros_like(pbc_ref)

    xs_raw = xs_ref[...]
    bc_raw = bc_ref[...]
    cw = cw_ref[...]
    cb = cb_ref[...]
    xs = _causal_conv(xs_raw, pxs_ref[...], cw[:, :SSM_WIDTH], cb[:, :SSM_WIDTH], None)
    bc = _causal_conv(bc_raw, pbc_ref[...], cw[:, SSM_WIDTH:], cb[:, SSM_WIDTH:], None)
    pxs_ref[...] = xs_raw
    pbc_ref[...] = bc_raw
    bm = bc[:, :N_SSM_GROUPS * D_STATE]
    cm = bc[:, N_SSM_GROUPS * D_STATE:]

    dtf, acum, acum_end = _ssd_scalars(misc_ref[...], dtb_ref[...], alog_ref[...], valid_ref[...],
                                       tri_ref[...], lsel_ref[...])
    row = lax.broadcasted_iota(jnp.int32, (CHUNK, CHUNK), 0)
    col = lax.broadcasted_iota(jnp.int32, (CHUNK, CHUNK), 1)
    y = _ssd_diag(xs, bm, cm, dtf, acum, col <= row)

    e = e_ref[...]
    exp_a = jnp.exp(_dot_exact(acum, e))
    w_end = _dot_exact(jnp.exp(acum_end - acum) * dtf, e)
    xw = xs * w_end
    ht = ht_ref[...]
    y_off = []
    new_states = []
    for g in range(N_SSM_GROUPS):
        gs = slice(g * D_STATE, (g + 1) * D_STATE)
        ws = slice(g * GROUP_WIDTH, (g + 1) * GROUP_WIDTH)
        y_off.append(_dot(cm[:, gs].astype(BF16), ht[:, ws].astype(BF16)))
        new_states.append(_dot(bm[:, gs].T.astype(BF16), xw[:, ws].astype(BF16)))
    y = y + jnp.concatenate(y_off, axis=1) * exp_a
    ht_new = exp_a[CHUNK - 1:CHUNK, :] * ht + jnp.concatenate(new_states, axis=1)
    ht_ref[...] = ht_new

    y = y + drow_ref[...] * xs
    z = z_ref[...]
    y = y * (z * _sigmoid(z))
    o_ref[...] = _rms(y, g_ref[...])

    @pl.when(c == chunks_per_seq - 1)
    def _():
        for j in range(SSM_WIDTH // LANES):
            st_ref[j * LANES:(j + 1) * LANES, :] = ht_new[:, j * LANES:(j + 1) * LANES].T


def _ssd_consts(l_rows_per_seq):
    r = lax.broadcasted_iota(jnp.int32, (CHUNK, CHUNK), 0)
    c = lax.broadcasted_iota(jnp.int32, (CHUNK, CHUNK), 1)
    same = (r // l_rows_per_seq) == (c // l_rows_per_seq)
    tri = (same & (c <= r)).astype(F32)
    lsel = (c == (r // l_rows_per_seq) * l_rows_per_seq + l_rows_per_seq - 1).astype(F32)
    return tri, lsel


def _expand_matrix():
    r = lax.broadcasted_iota(jnp.int32, (LANES, SSM_WIDTH), 0)
    c = lax.broadcasted_iota(jnp.int32, (LANES, SSM_WIDTH), 1)
    return (r == DT_LANE + c // SSM_HEADDIM).astype(F32)


def _row_spec(shape):
    return pl.BlockSpec(shape, lambda i: (0, 0))


def _ssd_prompt(proj, cw, cb, dtb, alog, valid, drow, g, e, batch, seq):
    chunks_per_seq = seq // CHUNK
    tri, lsel = _ssd_consts(CHUNK)
    kern = functools.partial(_ssd_prompt_kernel, chunks_per_seq=chunks_per_seq)
    return pl.pallas_call(
        kern,
        grid=(batch * chunks_per_seq,),
        in_specs=[
            pl.BlockSpec((CHUNK, SSM_WIDTH), lambda i: (i, COL_Z // SSM_WIDTH)),
            pl.BlockSpec((CHUNK, SSM_WIDTH), lambda i: (i, COL_XS // SSM_WIDTH)),
            pl.BlockSpec((CHUNK, BC_WIDTH), lambda i: (i, COL_BC // BC_WIDTH)),
            pl.BlockSpec((CHUNK, LANES), lambda i: (i, COL_MISC // LANES)),
            _row_spec((CONV_W, CONV_DIM)), _row_spec((1, CONV_DIM)),
            _row_spec((1, LANES)), _row_spec((1, LANES)), _row_spec((1, LANES)),
            _row_spec((1, SSM_WIDTH)), _row_spec((1, SSM_WIDTH)),
            _row_spec((CHUNK, CHUNK)), _row_spec((CHUNK, CHUNK)), _row_spec((LANES, SSM_WIDTH)),
        ],
        out_specs=[
            pl.BlockSpec((CHUNK, SSM_WIDTH), lambda i: (i, 0)),
            pl.BlockSpec((None, SSM_WIDTH, D_STATE), lambda i: (i // chunks_per_seq, 0, 0)),
        ],
        out_shape=[
            jax.ShapeDtypeStruct((batch * seq, SSM_WIDTH), F32),
            jax.ShapeDtypeStruct((batch, SSM_WIDTH, D_STATE), F32),
        ],
        scratch_shapes=[
            pltpu.VMEM((D_STATE, SSM_WIDTH), F32),
            pltpu.VMEM((CHUNK, SSM_WIDTH), F32),
            pltpu.VMEM((CHUNK, BC_WIDTH), F32),
        ],
        compiler_params=_params("arbitrary"),
        name="ssd_prompt",
    )(proj, proj, proj, proj, cw, cb, dtb, alog, valid, drow, g, tri, lsel, e)


def _ssd_sample_kernel(z_ref, xs_ref, bc_ref, misc_ref, prev_ref, h0_ref, cw_ref, cb_ref, dtb_ref,
                       alog_ref, valid_ref, drow_ref, g_ref, tri_ref, lsel_ref, e_ref,
                       o_ref, st_ref, *, n_seq, t_new):
    n_rows = n_seq * t_new
    live = lax.broadcasted_iota(jnp.int32, (CHUNK, 1), 0) < n_rows
    xs_raw = jnp.where(live, xs_ref[...], 0.0)
    bc_raw = jnp.where(live, bc_ref[...], 0.0)
    z = jnp.where(live, z_ref[...], 0.0)
    misc = jnp.where(live, misc_ref[...], 0.0)
    cw = cw_ref[...]
    cb = cb_ref[...]
    prev = prev_ref[...]
    xs = _causal_conv(xs_raw, prev[:, :SSM_WIDTH], cw[:, :SSM_WIDTH], cb[:, :SSM_WIDTH], t_new)
    bc = _causal_conv(bc_raw, prev[:, SSM_WIDTH:], cw[:, SSM_WIDTH:], cb[:, SSM_WIDTH:], t_new)
    bm = bc[:, :N_SSM_GROUPS * D_STATE]
    cm = bc[:, N_SSM_GROUPS * D_STATE:]

    dtf, acum, acum_end = _ssd_scalars(misc, dtb_ref[...], alog_ref[...], valid_ref[...],
                                       tri_ref[...], lsel_ref[...])
    row = lax.broadcasted_iota(jnp.int32, (CHUNK, CHUNK), 0)
    col = lax.broadcasted_iota(jnp.int32, (CHUNK, CHUNK), 1)
    visible = (row // t_new == col // t_new) & (col <= row)
    y = _ssd_diag(xs, bm, cm, dtf, acum, visible)

    e = e_ref[...]
    exp_a = jnp.exp(_dot_exact(acum, e))
    w_end = _dot_exact(jnp.exp(acum_end - acum) * dtf, e)
    xw = xs * w_end
    bm_t = [bm[:, g * D_STATE:(g + 1) * D_STATE].T for g in range(N_SSM_GROUPS)]
    lane_seq = lax.broadcasted_iota(jnp.int32, (D_STATE, CHUNK), 1) // t_new

    y_off_rows = []
    for b in range(n_seq):
        rows = slice(b * t_new, (b + 1) * t_new)
        h0 = h0_ref[b]
        ht = jnp.concatenate(
            [h0[j * LANES:(j + 1) * LANES, :].T for j in range(SSM_WIDTH // LANES)], axis=1)
        y_off = []
        new_states = []
        for g in range(N_SSM_GROUPS):
            gs = slice(g * D_STATE, (g + 1) * D_STATE)
            ws = slice(g * GROUP_WIDTH, (g + 1) * GROUP_WIDTH)
            y_off.append(_dot(cm[:, gs].astype(BF16), ht[:, ws].astype(BF16))[rows, :])
            bt = jnp.where(lane_seq == b, bm_t[g], 0.0)
            new_states.append(_dot(bt.astype(BF16), xw[:, ws].astype(BF16)))
        y_off_rows.append(jnp.concatenate(y_off, axis=1))
        last = (b + 1) * t_new - 1
        ht_new = exp_a[last:last + 1, :] * ht + jnp.concatenate(new_states, axis=1)
        for j in range(SSM_WIDTH // LANES):
            st_ref[b, j * LANES:(j + 1) * LANES, :] = ht_new[:, j * LANES:(j + 1) * LANES].T
    y_off_rows.append(jnp.zeros((CHUNK - n_rows, SSM_WIDTH), F32))
    y = y + jnp.concatenate(y_off_rows, axis=0) * exp_a

    y = y + drow_ref[...] * xs
    y = y * (z * _sigmoid(z))
    o_ref[...] = _rms(y, g_ref[...])[0:n_rows, :]


def _ssd_sample(proj, prev, h0, cw, cb, dtb, alog, valid, drow, g, e, row0, n_seq, t_new):
    tri, lsel = _ssd_consts(t_new)
    rb = row0 // CHUNK
    kern = functools.partial(_ssd_sample_kernel, n_seq=n_seq, t_new=t_new)
    full3 = pl.BlockSpec((n_seq, SSM_WIDTH, D_STATE), lambda i: (0, 0, 0))
    return pl.pallas_call(
        kern,
        grid=(1,),
        in_specs=[
            pl.BlockSpec((CHUNK, SSM_WIDTH), lambda i: (rb, COL_Z // SSM_WIDTH)),
            pl.BlockSpec((CHUNK, SSM_WIDTH), lambda i: (rb, COL_XS // SSM_WIDTH)),
            pl.BlockSpec((CHUNK, BC_WIDTH), lambda i: (rb, COL_BC // BC_WIDTH)),
            pl.BlockSpec((CHUNK, LANES), lambda i: (rb, COL_MISC // LANES)),
            _row_spec((CHUNK, CONV_DIM)),
            full3,
            _row_spec((CONV_W, CONV_DIM)), _row_spec((1, CONV_DIM)),
            _row_spec((1, LANES)), _row_spec((1, LANES)), _row_spec((1, LANES)),
            _row_spec((1, SSM_WIDTH)), _row_spec((1, SSM_WIDTH)),
            _row_spec((CHUNK, CHUNK)), _row_spec((CHUNK, CHUNK)), _row_spec((LANES, SSM_WIDTH)),
        ],
        out_specs=[
            pl.BlockSpec((n_seq * t_new, SSM_WIDTH), lambda i: (0, 0)),
            full3,
        ],
        out_shape=[
            jax.ShapeDtypeStruct((n_seq * t_new, SSM_WIDTH), F32),
            jax.ShapeDtypeStruct((n_seq, SSM_WIDTH, D_STATE), F32),
        ],
        compiler_params=_params("arbitrary"),
        name="ssd_sample",
    )(proj, proj, proj, proj, prev, h0, cw, cb, dtb, alog, valid, drow, g, tri, lsel, e)


def _out_proj_kernel(h_ref, a_ref, s_ref, wa_ref, ws_ref, o_ref):
    mixed = _dot(a_ref[...].astype(BF16), wa_ref[...]) + _dot(s_ref[...].astype(BF16), ws_ref[...])
    o_ref[...] = h_ref[...] + mixed


def _out_proj(h, o_att, o_ssm, w):
    t = h.shape[0]
    tm = ROW_ALIGN
    return pl.pallas_call(
        _out_proj_kernel,
        grid=(t // tm,),
        in_specs=[
            pl.BlockSpec((tm, D_MODEL), lambda i: (i, 0)),
            pl.BlockSpec((tm, ATT_WIDTH), lambda i: (i, 0)),
            pl.BlockSpec((tm, SSM_WIDTH), lambda i: (i, 0)),
            pl.BlockSpec((ATT_WIDTH, D_MODEL), lambda i: (0, 0)),
            pl.BlockSpec((SSM_WIDTH, D_MODEL), lambda i: (1, 0)),
        ],
        out_specs=pl.BlockSpec((tm, D_MODEL), lambda i: (i, 0)),
        out_shape=jax.ShapeDtypeStruct((t, D_MODEL), F32),
        compiler_params=_params("parallel"),
        name="out_proj",
    )(h, o_att, o_ssm, w, w)


def _mlp_kernel(h_ref, g_ref, wu_ref, wd_ref, o_ref, hn_ref):
    @pl.when(pl.program_id(1) == 0)
    def _():
        h = h_ref[...]
        hn_ref[...] = _rms(h, g_ref[...]).astype(BF16)
        o_ref[...] = h

    u = jnp.maximum(_dot(hn_ref[...], wu_ref[...]), 0.0)
    o_ref[...] += _dot((u * u).astype(BF16), wd_ref[...])


def _mlp(h, g, wu, wd):
    t = h.shape[0]
    tm = TOKEN_TILE
    tf = 512
    return pl.pallas_call(
        _mlp_kernel,
        grid=(t // tm, D_FF // tf),
        in_specs=[
            pl.BlockSpec((tm, D_MODEL), lambda i, f: (i, 0)),
            pl.BlockSpec((1, D_MODEL), lambda i, f: (0, 0)),
            pl.BlockSpec((D_MODEL, tf), lambda i, f: (0, f)),
            pl.BlockSpec((tf, D_MODEL), lambda i, f: (f, 0)),
        ],
        out_specs=pl.BlockSpec((tm, D_MODEL), lambda i, f: (i, 0)),
        out_shape=jax.ShapeDtypeStruct((t, D_MODEL), F32),
        scratch_shapes=[pltpu.VMEM((tm, D_MODEL), BF16)],
        compiler_params=_params("parallel", "arbitrary"),
        name="mlp",
    )(h, g, wu, wd)


def _ple_kernel(h_ref, p_ref, g_ref, wg_ref, wp_ref, gf_ref, o_ref, *, final):
    h = h_ref[...]
    gate = _sigmoid(_dot(_rms(h, g_ref[...]).astype(BF16), wg_ref[...]))
    out = h + _dot(p_ref[...].astype(BF16), wp_ref[...]) * gate
    if final:
        out = _rms(out, gf_ref[...])
    o_ref[...] = out


def _ple(h, p, g, wg, wp, g_final, final):
    t = h.shape[0]
    tm = ROW_ALIGN
    ple_dim = p.shape[1]
    return pl.pallas_call(
        functools.partial(_ple_kernel, final=final),
        grid=(t // tm,),
        in_specs=[
            pl.BlockSpec((tm, D_MODEL), lambda i: (i, 0)),
            pl.BlockSpec((tm, ple_dim), lambda i: (i, 0)),
            pl.BlockSpec((1, D_MODEL), lambda i: (0, 0)),
            pl.BlockSpec((D_MODEL, D_MODEL), lambda i: (0, 0)),
            pl.BlockSpec((ple_dim, D_MODEL), lambda i: (0, 0)),
            pl.BlockSpec((1, D_MODEL), lambda i: (0, 0)),
        ],
        out_specs=pl.BlockSpec((tm, D_MODEL), lambda i: (i, 0)),
        out_shape=jax.ShapeDtypeStruct((t, D_MODEL), F32),
        compiler_params=_params("parallel"),
        name="ple",
    )(h, p, g, wg, wp, g_final)


def _misc_row(values, lane0):
    row = jnp.zeros((1, LANES), F32)
    return row.at[0, lane0:lane0 + values.shape[0]].set(values.astype(F32))


def kernel(x_prompt, x_sample, cache_k, cache_v, cache_logf, state_ssm, state_conv, page_table, p_prompt, p_sample, norm_mix, w_in, b_forget, conv_w, conv_b, dt_bias, a_log, d_skip, norm_attn_out, norm_ssm_out, w_out, norm_mlp, w_up, w_down, norm_ple, w_ple_gate, w_ple_proj, norm_final):
    batch, seq, _ = x_prompt.shape
    n_seq, t_new, _ = x_sample.shape
    depth = w_in.shape[0]
    n_prompt = batch * seq
    n_sample = n_seq * t_new
    t_pad = -(-(n_prompt + CHUNK) // TOKEN_TILE) * TOKEN_TILE
    n_tail = t_pad - n_prompt - n_sample
    n_phys, page = cache_k.shape[1], cache_k.shape[2]

    h = jnp.concatenate([x_prompt.reshape(n_prompt, D_MODEL), x_sample.reshape(n_sample, D_MODEL),
                         jnp.zeros((n_tail, D_MODEL), F32)], axis=0)
    ple_dim = p_prompt.shape[-1]
    p_all = jnp.concatenate([p_prompt.reshape(depth, n_prompt, ple_dim),
                             p_sample.reshape(depth, n_sample, ple_dim),
                             jnp.zeros((depth, n_tail, ple_dim), F32)], axis=1)
    o_f = 3 * ATT_WIDTH
    o_z = o_f + N_ATT_HEADS
    o_xbc = o_z + SSM_WIDTH
    o_dt = o_xbc + CONV_DIM
    w_in_r = jnp.concatenate(
        [w_in[:, :, :o_f], w_in[:, :, o_z:o_xbc], w_in[:, :, o_xbc:o_dt], w_in[:, :, o_f:o_z],
         w_in[:, :, o_dt:], jnp.zeros((depth, D_MODEL, LANES - N_ATT_HEADS - N_SSM_HEADS), F32)],
        axis=2).astype(BF16)
    w_out_b = w_out.astype(BF16)
    w_up_b = w_up.astype(BF16)
    w_down_b = w_down.astype(BF16)
    w_gate_b = w_ple_gate.astype(BF16)
    w_proj_b = w_ple_proj.astype(BF16)

    tri_full, _ = _ssd_consts(CHUNK)
    tri_seq, _ = _ssd_consts(t_new)
    tri2 = jnp.stack([tri_full, tri_seq])
    expand = _expand_matrix()
    valid_row = _misc_row(jnp.ones((N_SSM_HEADS,), F32), DT_LANE)
    pbias = _past_bias(cache_logf, page_table)
    causal_new = jnp.arange(t_new)[None, :] <= jnp.arange(t_new)[:, None]

    outs = {name: [] for name in ("kp", "vp", "fp", "sp", "cp", "ks", "vs", "fs", "ss", "cs")}
    for l in range(depth):
        proj = _in_proj(h, norm_mix[l][None, :], w_in_r[l])
        logf, cum, cumt = _gates(proj, _misc_row(b_forget[l], F_LANE), tri2, n_prompt, seq)

        o_att_p = _fox_prompt(proj, cum, cumt, norm_attn_out[l][None, :], batch, seq)
        qc = cumt[:, n_prompt:n_prompt + n_sample].reshape(N_ATT_HEADS, n_seq, t_new)
        qc = jnp.transpose(qc, (1, 0, 2))
        qc_col = qc.reshape(n_seq, N_ATT_HEADS * t_new, 1)
        bias_new = jnp.where(causal_new[None, None], -qc[:, :, None, :], NEG)
        bias_new = jnp.concatenate(
            [bias_new.reshape(n_seq, N_ATT_HEADS * t_new, t_new),
             jnp.full((n_seq, N_ATT_HEADS * t_new, page - t_new), NEG, F32)], axis=2)
        o_att_s = _fox_sample(proj, qc_col, bias_new, pbias, cache_k, cache_v, page_table,
                              norm_attn_out[l][None, :], l, n_prompt, t_new)

        dtb = _misc_row(dt_bias[l], DT_LANE)
        alog = _misc_row(a_log[l], DT_LANE)
        drow = jnp.repeat(d_skip[l].astype(F32), SSM_HEADDIM)[None, :]
        ssm_args = (conv_w[l], conv_b[l][None, :], dtb, alog, valid_row, drow,
                    norm_ssm_out[l][None, :], expand)
        o_ssm_p, st_p = _ssd_prompt(proj, *ssm_args, batch, seq)
        prev = jnp.concatenate(
            [jnp.zeros((n_seq, t_new - (CONV_W - 1), CONV_DIM), F32), state_conv[l]], axis=1)
        prev = jnp.concatenate([prev.reshape(n_sample, CONV_DIM),
                                jnp.zeros((CHUNK - n_sample, CONV_DIM), F32)], axis=0)
        h0 = state_ssm[l].reshape(n_seq, SSM_WIDTH, D_STATE)
        o_ssm_s, st_s = _ssd_sample(proj, prev, h0, *ssm_args, n_prompt, n_seq, t_new)

        tail = jnp.zeros((n_tail, ATT_WIDTH), F32)
        o_att = jnp.concatenate([o_att_p, o_att_s, tail], axis=0)
        o_ssm = jnp.concatenate([o_ssm_p, o_ssm_s, tail], axis=0)
        h = _out_proj(h, o_att, o_ssm, w_out_b[l])
        h = _mlp(h, norm_mlp[l][None, :], w_up_b[l], w_down_b[l])
        h = _ple(h, p_all[l], norm_ple[l][None, :], w_gate_b[l], w_proj_b[l], norm_final[None, :],
                 final=(l == depth - 1))

        kv_shape_p = (batch, seq, N_ATT_HEADS, ATT_HEAD_DIM)
        kv_shape_s = (n_seq, t_new, N_ATT_HEADS, ATT_HEAD_DIM)
        sample_rows = slice(n_prompt, n_prompt + n_sample)
        outs["kp"].append(proj[:n_prompt, COL_K:COL_K + ATT_WIDTH].reshape(kv_shape_p))
        outs["vp"].append(proj[:n_prompt, COL_V:COL_V + ATT_WIDTH].reshape(kv_shape_p))
        outs["fp"].append(logf[:n_prompt].reshape(batch, seq, N_ATT_HEADS))
        outs["sp"].append(st_p.reshape(batch, N_SSM_HEADS, SSM_HEADDIM, D_STATE))
        xbc_p = proj[:n_prompt, COL_XS:COL_XS + CONV_DIM].reshape(batch, seq, CONV_DIM)
        outs["cp"].append(xbc_p[:, seq - (CONV_W - 1):, :])
        outs["ks"].append(proj[sample_rows, COL_K:COL_K + ATT_WIDTH].reshape(kv_shape_s))
        outs["vs"].append(proj[sample_rows, COL_V:COL_V + ATT_WIDTH].reshape(kv_shape_s))
        outs["fs"].append(logf[sample_rows].reshape(n_seq, t_new, N_ATT_HEADS))
        outs["ss"].append(st_s.reshape(n_seq, N_SSM_HEADS, SSM_HEADDIM, D_STATE))
        xbc_s = proj[sample_rows, COL_XS:COL_XS + CONV_DIM].reshape(n_seq, t_new, CONV_DIM)
        outs["cs"].append(xbc_s[:, t_new - (CONV_W - 1):, :])

    y_prompt = h[:n_prompt].reshape(batch, seq, D_MODEL)
    y_sample = h[n_prompt:n_prompt + n_sample].reshape(n_seq, t_new, D_MODEL)
    stack = lambda name: jnp.stack(outs[name])
    return (y_prompt, y_sample, stack("kp"), stack("vp"), stack("fp"), stack("sp"), stack("cp"),
            stack("ks"), stack("vs"), stack("fs"), stack("ss"), stack("cs"))
```

```python
import functools

import numpy as np
import jax
import jax.numpy as jnp
from jax import lax
from jax.experimental import pallas as pl
from jax.experimental.pallas import tpu as pltpu

F32 = jnp.float32
BF16 = jnp.bfloat16
HIGHEST = lax.Precision.HIGHEST

D_MODEL = 2048
ATT_HEAD_DIM = 128
ATT_WIDTH = D_MODEL // 2
N_ATT_HEADS = ATT_WIDTH // ATT_HEAD_DIM
ATT_SCALE = ATT_HEAD_DIM ** -0.5
SSM_WIDTH = D_MODEL - ATT_WIDTH
SSM_HEADDIM = 64
N_SSM_HEADS = SSM_WIDTH // SSM_HEADDIM
N_SSM_GROUPS = 2
GROUP_WIDTH = SSM_WIDTH // N_SSM_GROUPS
D_STATE = 128
CONV_W = 4
BC_WIDTH = 2 * N_SSM_GROUPS * D_STATE
CONV_DIM = SSM_WIDTH + BC_WIDTH
D_FF = 4 * D_MODEL
RMS_EPS = 1e-6
NEG = -1e30

SUBLANES = 8
LANES = 128
VMEM_LIMIT = 56 * 1024 * 1024

COL_Q = 0
COL_K = ATT_WIDTH
COL_V = 2 * ATT_WIDTH
COL_Z = 3 * ATT_WIDTH
COL_XS = COL_Z + SSM_WIDTH
COL_BC = COL_XS + SSM_WIDTH
COL_MISC = COL_BC + BC_WIDTH
PROJ_TILE = ATT_WIDTH
PROJ_W = 6 * PROJ_TILE
F_LANE = 0
DT_LANE = N_ATT_HEADS

CHUNK = 128
TOKEN_TILE = 768
ROW_ALIGN = 256


def _rms(x, g):
    ms = jnp.mean(x * x, axis=-1, keepdims=True)
    return (x * lax.rsqrt(ms + RMS_EPS)) * g


def _sigmoid(x):
    return 1.0 / (1.0 + jnp.exp(-x))


def _softplus(x):
    return jnp.maximum(x, 0.0) + jnp.log1p(jnp.exp(-jnp.abs(x)))


def _dot(a, b):
    return jnp.dot(a, b, preferred_element_type=F32)


def _dot_exact(a, b):
    return jnp.dot(a, b, precision=HIGHEST, preferred_element_type=F32)


def _dot_nt(a, b):
    return lax.dot_general(a, b, (((1,), (1,)), ((), ())), preferred_element_type=F32)


def _tile_lanes(x, rep):
    return x if rep == 1 else jnp.concatenate([x] * rep, axis=1)


def _params(*semantics):
    return pltpu.CompilerParams(dimension_semantics=semantics, vmem_limit_bytes=VMEM_LIMIT)


def _in_proj_kernel(x_ref, g_ref, w_ref, *rest, n_prompt, n_sample):
    o_ref, kp_ref, vp_ref, ks_ref, vs_ref, xn_ref = rest
    i = pl.program_id(0)
    j = pl.program_id(1)
    tm = x_ref.shape[0]

    @pl.when(j == 0)
    def _():
        xn_ref[...] = _rms(x_ref[...], g_ref[...]).astype(BF16)

    res = _dot(xn_ref[...], w_ref[...])
    o_ref[...] = res

    def emit(prompt_ref, sample_ref):
        prompt_ref[...] = res.reshape(tm, N_ATT_HEADS, ATT_HEAD_DIM)

        @pl.when(i == n_prompt // tm)
        def _():
            r0 = n_prompt % tm
            sample_ref[...] = res[r0:r0 + n_sample, :].reshape(n_sample, N_ATT_HEADS, ATT_HEAD_DIM)

    @pl.when(j == COL_K // PROJ_TILE)
    def _():
        emit(kp_ref, ks_ref)

    @pl.when(j == COL_V // PROJ_TILE)
    def _():
        emit(vp_ref, vs_ref)


def _in_proj(h, g, w, n_prompt, n_sample):
    t = h.shape[0]
    tm = TOKEN_TILE
    assert n_prompt % tm + n_sample <= tm
    head_shape = (N_ATT_HEADS, ATT_HEAD_DIM)
    cache_shapes = [(t,) + head_shape] * 2 + [(n_sample,) + head_shape] * 2
    cache_specs = [pl.BlockSpec((tm,) + head_shape, lambda i, j: (i, 0, 0))] * 2 + \
                  [pl.BlockSpec((n_sample,) + head_shape, lambda i, j: (0, 0, 0))] * 2
    return pl.pallas_call(
        functools.partial(_in_proj_kernel, n_prompt=n_prompt, n_sample=n_sample),
        grid=(t // tm, PROJ_W // PROJ_TILE),
        in_specs=[
            pl.BlockSpec((tm, D_MODEL), lambda i, j: (i, 0)),
            pl.BlockSpec((1, D_MODEL), lambda i, j: (0, 0)),
            pl.BlockSpec((D_MODEL, PROJ_TILE), lambda i, j: (0, j)),
        ],
        out_specs=[pl.BlockSpec((tm, PROJ_TILE), lambda i, j: (i, j))] + cache_specs,
        out_shape=[jax.ShapeDtypeStruct((t, PROJ_W), F32)]
                  + [jax.ShapeDtypeStruct(s, F32) for s in cache_shapes],
        scratch_shapes=[pltpu.VMEM((tm, D_MODEL), BF16)],
        compiler_params=_params("arbitrary", "arbitrary"),
        name="in_proj",
    )(h, g, w)


def _gates_kernel(misc_ref, fb_ref, tri_ref, logf_ref, cum_ref, cumt_ref, carry_ref, *,
                  chunks_per_seq, n_prompt_chunks):
    i = pl.program_id(0)

    @pl.when(jnp.logical_or(lax.rem(i, chunks_per_seq) == 0, i >= n_prompt_chunks))
    def _():
        carry_ref[...] = jnp.zeros_like(carry_ref)

    x = misc_ref[...] + fb_ref[...]
    lf = jnp.minimum(x, 0.0) - jnp.log1p(jnp.exp(-jnp.abs(x)))
    c = _dot_exact(tri_ref[...], lf) + carry_ref[...]
    carry_ref[...] = c[CHUNK - 1:CHUNK, :]
    logf_ref[...] = lf[:, F_LANE:F_LANE + N_ATT_HEADS]
    cum_ref[...] = c[:, F_LANE:F_LANE + N_ATT_HEADS]
    cumt_ref[...] = c.T[F_LANE:F_LANE + N_ATT_HEADS, :]


def _gates(proj, fb_row, tri2, n_prompt_rows, seq):
    n_prompt_chunks = n_prompt_rows // CHUNK
    n_chunks = n_prompt_chunks + 1
    rows = n_chunks * CHUNK
    kern = functools.partial(_gates_kernel, chunks_per_seq=seq // CHUNK,
                             n_prompt_chunks=n_prompt_chunks)
    return pl.pallas_call(
        kern,
        grid=(n_chunks,),
        in_specs=[
            pl.BlockSpec((CHUNK, LANES), lambda i: (i, COL_MISC // LANES)),
            pl.BlockSpec((1, LANES), lambda i: (0, 0)),
            pl.BlockSpec((None, CHUNK, CHUNK), lambda i: (i // n_prompt_chunks, 0, 0)),
        ],
        out_specs=[
            pl.BlockSpec((CHUNK, N_ATT_HEADS), lambda i: (i, 0)),
            pl.BlockSpec((CHUNK, N_ATT_HEADS), lambda i: (i, 0)),
            pl.BlockSpec((N_ATT_HEADS, CHUNK), lambda i: (0, i)),
        ],
        out_shape=[
            jax.ShapeDtypeStruct((rows, N_ATT_HEADS), F32),
            jax.ShapeDtypeStruct((rows, N_ATT_HEADS), F32),
            jax.ShapeDtypeStruct((N_ATT_HEADS, rows), F32),
        ],
        scratch_shapes=[pltpu.VMEM((1, LANES), F32)],
        compiler_params=_params("arbitrary"),
        name="gates",
    )(proj, fb_row, tri2)


def _fox_prompt_kernel(qi_ref, ki_ref, q_ref, k_ref, v_ref, cq_ref, ck_ref, g_ref, o_ref,
                       m_ref, l_ref, acc_ref, cqb_ref, *, tq, tk):
    p = pl.program_id(1)
    qi = qi_ref[p]
    ki = ki_ref[p]
    rep = tk // LANES

    @pl.when(ki == 0)
    def _():
        m_ref[...] = jnp.full_like(m_ref, NEG)
        l_ref[...] = jnp.zeros_like(l_ref)
        acc_ref[...] = jnp.zeros_like(acc_ref)
        cq = cq_ref[...]
        for h in range(N_ATT_HEADS):
            cqb_ref[h] = jnp.broadcast_to(cq[:, h:h + 1], (tq, LANES))

    def step(on_diagonal):
        ck = ck_ref[...]
        if on_diagonal:
            visible = (lax.broadcasted_iota(jnp.int32, (tq, tk), 1)
                       <= lax.broadcasted_iota(jnp.int32, (tq, tk), 0))
        for h in range(N_ATT_HEADS):
            sl = slice(h * ATT_HEAD_DIM, (h + 1) * ATT_HEAD_DIM)
            s = _dot_nt(q_ref[:, sl].astype(BF16), k_ref[:, sl].astype(BF16))
            s = s * ATT_SCALE + (_tile_lanes(cqb_ref[h], rep) - ck[h:h + 1, :])
            if on_diagonal:
                s = jnp.where(visible, s, NEG)
            m_prev = m_ref[h]
            m_new = jnp.maximum(m_prev, jnp.max(s, axis=-1, keepdims=True))
            alpha = jnp.exp(m_prev - m_new)
            pm = jnp.exp(s - _tile_lanes(m_new, rep))
            l_ref[h] = alpha * l_ref[h] + jnp.sum(pm, axis=-1, keepdims=True)
            acc_ref[:, sl] = alpha * acc_ref[:, sl] + _dot(pm.astype(BF16), v_ref[:, sl].astype(BF16))
            m_ref[h] = m_new

    @pl.when(ki < qi)
    def _():
        step(False)

    @pl.when(ki == qi)
    def _():
        step(True)
        for h in range(N_ATT_HEADS):
            sl = slice(h * ATT_HEAD_DIM, (h + 1) * ATT_HEAD_DIM)
            acc_ref[:, sl] = acc_ref[:, sl] / l_ref[h]
        o_ref[...] = _rms(acc_ref[...], g_ref[...])


def _fox_prompt(proj, cum, cumt, g, batch, seq):
    tq = tk = 256
    nq = seq // tq
    pairs = [(a, b) for a in range(nq) for b in range(a + 1)]
    qi_tab = jnp.asarray(np.array([a for a, _ in pairs], np.int32))
    ki_tab = jnp.asarray(np.array([b for _, b in pairs], np.int32))
    kern = functools.partial(_fox_prompt_kernel, tq=tq, tk=tk)
    grid_spec = pltpu.PrefetchScalarGridSpec(
        num_scalar_prefetch=2,
        grid=(batch, len(pairs)),
        in_specs=[
            pl.BlockSpec((tq, ATT_WIDTH), lambda b, p, qt, kt: (b * nq + qt[p], COL_Q // ATT_WIDTH)),
            pl.BlockSpec((tk, ATT_WIDTH), lambda b, p, qt, kt: (b * nq + kt[p], COL_K // ATT_WIDTH)),
            pl.BlockSpec((tk, ATT_WIDTH), lambda b, p, qt, kt: (b * nq + kt[p], COL_V // ATT_WIDTH)),
            pl.BlockSpec((tq, N_ATT_HEADS), lambda b, p, qt, kt: (b * nq + qt[p], 0)),
            pl.BlockSpec((N_ATT_HEADS, tk), lambda b, p, qt, kt: (0, b * nq + kt[p])),
            pl.BlockSpec((1, ATT_WIDTH), lambda b, p, qt, kt: (0, 0)),
        ],
        out_specs=pl.BlockSpec((tq, ATT_WIDTH), lambda b, p, qt, kt: (b * nq + qt[p], 0)),
        scratch_shapes=[
            pltpu.VMEM((N_ATT_HEADS, tq, LANES), F32),
            pltpu.VMEM((N_ATT_HEADS, tq, LANES), F32),
            pltpu.VMEM((tq, ATT_WIDTH), F32),
            pltpu.VMEM((N_ATT_HEADS, tq, LANES), F32),
        ],
    )
    return pl.pallas_call(
        kern,
        grid_spec=grid_spec,
        out_shape=jax.ShapeDtypeStruct((batch * seq, ATT_WIDTH), F32),
        compiler_params=_params("parallel", "arbitrary"),
        name="fox_prompt",
    )(qi_tab, ki_tab, proj, proj, proj, cum, cumt, g)


def _past_bias_kernel(pt_ref, logf_hbm, m1_ref, s_ref, su_ref, e8_ref, o_ref, x_ref, sem, *,
                      n_pages):
    l = pl.program_id(0)
    b = pl.program_id(1)

    def page_copy(p):
        page = pt_ref[b * n_pages + p]
        return pltpu.make_async_copy(logf_hbm.at[l, pl.ds(page, 1), :], x_ref.at[pl.ds(p, 1), :], sem)

    def start(p, carry):
        page_copy(p).start()
        return carry

    def wait(p, carry):
        page_copy(p).wait()
        return carry

    lax.fori_loop(0, n_pages, start, 0)
    lax.fori_loop(0, n_pages, wait, 0)
    x = x_ref[...]
    within = _dot_exact(x, m1_ref[...])
    totals = _dot_exact(x, s_ref[...])
    later = _dot_exact(su_ref[...], totals)
    o_ref[...] = within + _dot_exact(later, e8_ref[...])


def _past_bias(cache_logf, page_table):
    depth, n_phys, page, heads = cache_logf.shape
    n_seq, n_pages = page_table.shape
    width = page * heads
    r = lax.broadcasted_iota(jnp.int32, (width, width), 0)
    c = lax.broadcasted_iota(jnp.int32, (width, width), 1)
    m1 = ((r % heads == c % heads) & (r // heads > c // heads)).astype(F32)
    r = lax.broadcasted_iota(jnp.int32, (width, LANES), 0)
    c = lax.broadcasted_iota(jnp.int32, (width, LANES), 1)
    s = (r % heads == c).astype(F32)
    r = lax.broadcasted_iota(jnp.int32, (n_pages, n_pages), 0)
    c = lax.broadcasted_iota(jnp.int32, (n_pages, n_pages), 1)
    su = (c > r).astype(F32)
    r = lax.broadcasted_iota(jnp.int32, (LANES, width), 0)
    c = lax.broadcasted_iota(jnp.int32, (LANES, width), 1)
    e8 = (r == c % heads).astype(F32)
    const = lambda shape: pl.BlockSpec(shape, lambda l, b, pt: (0, 0))
    grid_spec = pltpu.PrefetchScalarGridSpec(
        num_scalar_prefetch=1,
        grid=(depth, n_seq),
        in_specs=[
            pl.BlockSpec(memory_space=pl.ANY),
            const((width, width)), const((width, LANES)), const((n_pages, n_pages)),
            const((LANES, width)),
        ],
        out_specs=pl.BlockSpec((None, None, n_pages, width), lambda l, b, pt: (l, b, 0, 0)),
        scratch_shapes=[pltpu.VMEM((n_pages, width), F32), pltpu.SemaphoreType.DMA(())],
    )
    out = pl.pallas_call(
        functools.partial(_past_bias_kernel, n_pages=n_pages),
        grid_spec=grid_spec,
        out_shape=jax.ShapeDtypeStruct((depth, n_seq, n_pages, width), F32),
        compiler_params=_params("arbitrary", "arbitrary"),
        name="past_bias",
    )(page_table.reshape(-1), cache_logf.reshape(depth, n_phys, width), m1, s, su, e8)
    return out


def _fox_sample_kernel(pt_ref, q_ref, kn_ref, vn_ref, qc_ref, bn_ref, pb_ref, g_ref, *rest,
                       pages_per_step, t_new):
    k_refs = rest[:pages_per_step]
    v_refs = rest[pages_per_step:2 * pages_per_step]
    o_ref, qall_ref, qcm_ref, kpad_ref, vpad_ref, m_ref, l_ref, acc_ref = rest[2 * pages_per_step:]
    p = pl.program_id(1)
    rows, width = qcm_ref.shape

    @pl.when(p == 0)
    def _():
        m_ref[...] = jnp.full_like(m_ref, NEG)
        l_ref[...] = jnp.zeros_like(l_ref)
        acc_ref[...] = jnp.zeros_like(acc_ref)
        for h in range(N_ATT_HEADS):
            qall_ref[h * t_new:(h + 1) * t_new, :] = q_ref[:, h * ATT_HEAD_DIM:(h + 1) * ATT_HEAD_DIM]
        head_of_row = lax.broadcasted_iota(jnp.int32, (rows, width), 0) // t_new
        head_of_lane = lax.rem(lax.broadcasted_iota(jnp.int32, (rows, width), 1), N_ATT_HEADS)
        qcm_ref[...] = qc_ref[...] + jnp.where(head_of_row == head_of_lane, 0.0, NEG)

    qall = qall_ref[...].astype(BF16)

    def attend(pages):
        scores = [_dot_nt(qall, k_flat.astype(BF16)) * ATT_SCALE + bias for k_flat, _, bias in pages]
        m_cur = functools.reduce(jnp.maximum, [jnp.max(s, axis=-1, keepdims=True) for s in scores])
        m_prev = m_ref[...]
        m_new = jnp.maximum(m_prev, m_cur)
        alpha = jnp.exp(m_prev - m_new)
        l_new = alpha * l_ref[...]
        acc_new = alpha * acc_ref[...]
        for s, (_, v_flat, _) in zip(scores, pages):
            pm = jnp.exp(s - _tile_lanes(m_new, s.shape[1] // LANES))
            l_new = l_new + jnp.sum(pm, axis=-1, keepdims=True)
            acc_new = acc_new + _dot(pm.astype(BF16), v_flat.astype(BF16))
        l_ref[...] = l_new
        acc_ref[...] = acc_new
        m_ref[...] = m_new

    qcm = qcm_ref[...]
    attend([(k_refs[i][...].reshape(width, ATT_HEAD_DIM), v_refs[i][...].reshape(width, ATT_HEAD_DIM),
             qcm + pb_ref[i:i + 1, :]) for i in range(pages_per_step)])

    @pl.when(p == pl.num_programs(1) - 1)
    def _():
        n_new = kn_ref.shape[0]
        kpad_ref[...] = jnp.zeros_like(kpad_ref)
        vpad_ref[...] = jnp.zeros_like(vpad_ref)
        kpad_ref[0:n_new, :] = kn_ref[...]
        vpad_ref[0:n_new, :] = vn_ref[...]
        attend([(kpad_ref[...], vpad_ref[...], qc_ref[...] + bn_ref[...])])
        o_full = acc_ref[...] / l_ref[...]
        o = jnp.concatenate([o_full[h * t_new:(h + 1) * t_new, :] for h in range(N_ATT_HEADS)], axis=1)
        o_ref[...] = _rms(o, g_ref[...])


def _fox_sample(proj, k_new, v_new, qc_col, bias_new, pbias, cache_k, cache_v, page_table, g, layer,
                row0, t_new):
    n_seq, n_pages = page_table.shape
    page = cache_k.shape[2]
    width = page * N_ATT_HEADS
    pages_per_step = 8
    steps = n_pages // pages_per_step
    rows = N_ATT_HEADS * t_new
    n_new = t_new * N_ATT_HEADS
    rb0 = row0 // t_new

    def page_spec(i):
        return pl.BlockSpec(
            (None, None, page, N_ATT_HEADS, ATT_HEAD_DIM),
            lambda b, p, pt: (layer, pt[b * n_pages + p * pages_per_step + i], 0, 0, 0))

    per_seq = lambda shape: pl.BlockSpec((None,) + shape, lambda b, p, pt: (b, 0, 0))
    new_rows = lambda: per_seq((n_new, ATT_HEAD_DIM))
    grid_spec = pltpu.PrefetchScalarGridSpec(
        num_scalar_prefetch=1,
        grid=(n_seq, steps),
        in_specs=[
            pl.BlockSpec((t_new, ATT_WIDTH), lambda b, p, pt: (rb0 + b, COL_Q // ATT_WIDTH)),
            new_rows(), new_rows(),
            per_seq((rows, 1)), per_seq((rows, LANES)),
            pl.BlockSpec((None, None, pages_per_step, width), lambda b, p, pt: (layer, b, p, 0)),
            pl.BlockSpec((1, ATT_WIDTH), lambda b, p, pt: (0, 0)),
        ] + [page_spec(i) for i in range(pages_per_step)] * 2,
        out_specs=pl.BlockSpec((t_new, ATT_WIDTH), lambda b, p, pt: (b, 0)),
        scratch_shapes=[
            pltpu.VMEM((rows, ATT_HEAD_DIM), F32),
            pltpu.VMEM((rows, width), F32),
            pltpu.VMEM((LANES, ATT_HEAD_DIM), F32),
            pltpu.VMEM((LANES, ATT_HEAD_DIM), F32),
            pltpu.VMEM((rows, LANES), F32),
            pltpu.VMEM((rows, LANES), F32),
            pltpu.VMEM((rows, ATT_HEAD_DIM), F32),
        ],
    )
    kern = functools.partial(_fox_sample_kernel, pages_per_step=pages_per_step, t_new=t_new)
    return pl.pallas_call(
        kern,
        grid_spec=grid_spec,
        out_shape=jax.ShapeDtypeStruct((n_seq * t_new, ATT_WIDTH), F32),
        compiler_params=_params("parallel", "arbitrary"),
        name="fox_sample",
    )(page_table.reshape(-1), proj, k_new, v_new, qc_col, bias_new, pbias, g,
      *([cache_k] * pages_per_step), *([cache_v] * pages_per_step))


def _causal_conv(x, prev, w, b, rows_per_seq):
    n = x.shape[0]
    row = lax.broadcasted_iota(jnp.int32, x.shape, 0)
    out = None
    for j in range(CONV_W - 1, 0, -1):
        if rows_per_seq is None:
            shifted = jnp.where(row >= j, pltpu.roll(x, j, 0), pltpu.roll(prev, j, 0))
        else:
            shifted = jnp.where(lax.rem(row, rows_per_seq) >= j, pltpu.roll(x, j, 0),
                                pltpu.roll(prev, (j - rows_per_seq) % n, 0))
        tap = CONV_W - 1 - j
        term = shifted * w[tap:tap + 1, :]
        out = term if out is None else out + term
    out = out + x * w[CONV_W - 1:CONV_W, :]
    out = out + b
    return out * _sigmoid(out)


def _ssd_scalars(misc, dtb, alog, valid, tri, lsel):
    dtf = _softplus(misc + dtb)
    a = dtf * (-(jnp.exp(alog) * valid))
    acum = _dot_exact(tri, a)
    acum_end = _dot_exact(lsel, acum)
    return dtf, acum, acum_end


def _ssd_diag(xs, bm, cm, dtf, acum, visible):
    acum_t = acum.T
    dt_t = dtf.T
    lane = lax.broadcasted_iota(jnp.int32, (xs.shape[0], LANES), 1)
    heads_per_group = N_SSM_HEADS // N_SSM_GROUPS
    blocks = []
    for g in range(N_SSM_GROUPS):
        gs = slice(g * D_STATE, (g + 1) * D_STATE)
        cb = _dot_nt(cm[:, gs].astype(BF16), bm[:, gs].astype(BF16))
        for j in range(heads_per_group // 2):
            blk = g * (heads_per_group // 2) + j
            xpair = xs[:, blk * LANES:(blk + 1) * LANES]
            y = None
            for hh in range(2):
                ch = DT_LANE + 2 * blk + hh
                seg = acum[:, ch:ch + 1] - acum_t[ch:ch + 1, :]
                decay = jnp.exp(jnp.where(visible, seg, NEG))
                m = (cb * decay) * dt_t[ch:ch + 1, :]
                in_half = (lane >= SSM_HEADDIM) if hh else (lane < SSM_HEADDIM)
                xh = jnp.where(in_half, xpair, 0.0)
                part = _dot(m.astype(BF16), xh.astype(BF16))
                y = part if y is None else y + part
            blocks.append(y)
    return jnp.concatenate(blocks, axis=1)


def _ssd_prompt_kernel(z_ref, xs_ref, bc_ref, misc_ref, cw_ref, cb_ref, dtb_ref, alog_ref,
                       valid_ref, drow_ref, g_ref, tri_ref, lsel_ref, e_ref,
                       o_ref, st_ref, ht_ref, pxs_ref, pbc_ref, *, chunks_per_seq):
    c = lax.rem(pl.program_id(0), chunks_per_seq)

    @pl.when(c == 0)
    def _():
        ht_ref[...] = jnp.zeros_like(ht_ref)
        pxs_ref[...] = jnp.zeros_like(pxs_ref)
        pbc_ref[...] = jnp.zeros_like(pbc_ref)

    xs_raw = xs_ref[...]
    bc_raw = bc_ref[...]
    cw = cw_ref[...]
    cb = cb_ref[...]
    xs = _causal_conv(xs_raw, pxs_ref[...], cw[:, :SSM_WIDTH], cb[:, :SSM_WIDTH], None)
    bc = _causal_conv(bc_raw, pbc_ref[...], cw[:, SSM_WIDTH:], cb[:, SSM_WIDTH:], None)
    pxs_ref[...] = xs_raw
    pbc_ref[...] = bc_raw
    bm = bc[:, :N_SSM_GROUPS * D_STATE]
    cm = bc[:, N_SSM_GROUPS * D_STATE:]

    dtf, acum, acum_end = _ssd_scalars(misc_ref[...], dtb_ref[...], alog_ref[...], valid_ref[...],
                                       tri_ref[...], lsel_ref[...])
    row = lax.broadcasted_iota(jnp.int32, (CHUNK, CHUNK), 0)
    col = lax.broadcasted_iota(jnp.int32, (CHUNK, CHUNK), 1)
    y = _ssd_diag(xs, bm, cm, dtf, acum, col <= row)

    e = e_ref[...]
    exp_a = jnp.exp(_dot_exact(acum, e))
    w_end = _dot_exact(jnp.exp(acum_end - acum) * dtf, e)
    xw = xs * w_end
    ht = ht_ref[...]
    y_off = []
    new_states = []
    for g in range(N_SSM_GROUPS):
        gs = slice(g * D_STATE, (g + 1) * D_STATE)
        ws = slice(g * GROUP_WIDTH, (g + 1) * GROUP_WIDTH)
        y_off.append(_dot(cm[:, gs].astype(BF16), ht[:, ws].astype(BF16)))
        new_states.append(_dot(bm[:, gs].T.astype(BF16), xw[:, ws].astype(BF16)))
    y = y + jnp.concatenate(y_off, axis=1) * exp_a
    ht_new = exp_a[CHUNK - 1:CHUNK, :] * ht + jnp.concatenate(new_states, axis=1)
    ht_ref[...] = ht_new

    y = y + drow_ref[...] * xs
    z = z_ref[...]
    y = y * (z * _sigmoid(z))
    o_ref[...] = _rms(y, g_ref[...])

    @pl.when(c == chunks_per_seq - 1)
    def _():
        for j in range(SSM_WIDTH // LANES):
            st_ref[j * LANES:(j + 1) * LANES, :] = ht_new[:, j * LANES:(j + 1) * LANES].T


def _ssd_consts(l_rows_per_seq):
    r = lax.broadcasted_iota(jnp.int32, (CHUNK, CHUNK), 0)
    c = lax.broadcasted_iota(jnp.int32, (CHUNK, CHUNK), 1)
    same = (r // l_rows_per_seq) == (c // l_rows_per_seq)
    tri = (same & (c <= r)).astype(F32)
    lsel = (c == (r // l_rows_per_seq) * l_rows_per_seq + l_rows_per_seq - 1).astype(F32)
    return tri, lsel


def _expand_matrix():
    r = lax.broadcasted_iota(jnp.int32, (LANES, SSM_WIDTH), 0)
    c = lax.broadcasted_iota(jnp.int32, (LANES, SSM_WIDTH), 1)
    return (r == DT_LANE + c // SSM_HEADDIM).astype(F32)


def _row_spec(shape):
    return pl.BlockSpec(shape, lambda i: (0, 0))


def _ssd_prompt(proj, cw, cb, dtb, alog, valid, drow, g, e, batch, seq):
    chunks_per_seq = seq // CHUNK
    tri, lsel = _ssd_consts(CHUNK)
    kern = functools.partial(_ssd_prompt_kernel, chunks_per_seq=chunks_per_seq)
    return pl.pallas_call(
        kern,
        grid=(batch * chunks_per_seq,),
        in_specs=[
            pl.BlockSpec((CHUNK, SSM_WIDTH), lambda i: (i, COL_Z // SSM_WIDTH)),
            pl.BlockSpec((CHUNK, SSM_WIDTH), lambda i: (i, COL_XS // SSM_WIDTH)),
            pl.BlockSpec((CHUNK, BC_WIDTH), lambda i: (i, COL_BC // BC_WIDTH)),
            pl.BlockSpec((CHUNK, LANES), lambda i: (i, COL_MISC // LANES)),
            _row_spec((CONV_W, CONV_DIM)), _row_spec((1, CONV_DIM)),
            _row_spec((1, LANES)), _row_spec((1, LANES)), _row_spec((1, LANES)),
            _row_spec((1, SSM_WIDTH)), _row_spec((1, SSM_WIDTH)),
            _row_spec((CHUNK, CHUNK)), _row_spec((CHUNK, CHUNK)), _row_spec((LANES, SSM_WIDTH)),
        ],
        out_specs=[
            pl.BlockSpec((CHUNK, SSM_WIDTH), lambda i: (i, 0)),
            pl.BlockSpec((None, SSM_WIDTH, D_STATE), lambda i: (i // chunks_per_seq, 0, 0)),
        ],
        out_shape=[
            jax.ShapeDtypeStruct((batch * seq, SSM_WIDTH), F32),
            jax.ShapeDtypeStruct((batch, SSM_WIDTH, D_STATE), F32),
        ],
        scratch_shapes=[
            pltpu.VMEM((D_STATE, SSM_WIDTH), F32),
            pltpu.VMEM((CHUNK, SSM_WIDTH), F32),
            pltpu.VMEM((CHUNK, BC_WIDTH), F32),
        ],
        compiler_params=_params("arbitrary"),
        name="ssd_prompt",
    )(proj, proj, proj, proj, cw, cb, dtb, alog, valid, drow, g, tri, lsel, e)


def _ssd_sample_kernel(z_ref, xs_ref, bc_ref, misc_ref, prev_ref, h0_ref, cw_ref, cb_ref, dtb_ref,
                       alog_ref, valid_ref, drow_ref, g_ref, tri_ref, lsel_ref, e_ref,
                       o_ref, st_ref, *, n_seq, t_new):
    n_rows = n_seq * t_new
    live = lax.broadcasted_iota(jnp.int32, (CHUNK, 1), 0) < n_rows
    xs_raw = jnp.where(live, xs_ref[...], 0.0)
    bc_raw = jnp.where(live, bc_ref[...], 0.0)
    z = jnp.where(live, z_ref[...], 0.0)
    misc = jnp.where(live, misc_ref[...], 0.0)
    cw = cw_ref[...]
    cb = cb_ref[...]
    prev = prev_ref[...]
    xs = _causal_conv(xs_raw, prev[:, :SSM_WIDTH], cw[:, :SSM_WIDTH], cb[:, :SSM_WIDTH], t_new)
    bc = _causal_conv(bc_raw, prev[:, SSM_WIDTH:], cw[:, SSM_WIDTH:], cb[:, SSM_WIDTH:], t_new)
    bm = bc[:, :N_SSM_GROUPS * D_STATE]
    cm = bc[:, N_SSM_GROUPS * D_STATE:]

    dtf, acum, acum_end = _ssd_scalars(misc, dtb_ref[...], alog_ref[...], valid_ref[...],
                                       tri_ref[...], lsel_ref[...])
    row = lax.broadcasted_iota(jnp.int32, (CHUNK, CHUNK), 0)
    col = lax.broadcasted_iota(jnp.int32, (CHUNK, CHUNK), 1)
    visible = (row // t_new == col // t_new) & (col <= row)
    y = _ssd_diag(xs, bm, cm, dtf, acum, visible)

    e = e_ref[...]
    exp_a = jnp.exp(_dot_exact(acum, e))
    w_end = _dot_exact(jnp.exp(acum_end - acum) * dtf, e)
    xw = xs * w_end
    bm_t = [bm[:, g * D_STATE:(g + 1) * D_STATE].T for g in range(N_SSM_GROUPS)]
    lane_seq = lax.broadcasted_iota(jnp.int32, (D_STATE, CHUNK), 1) // t_new

    y_off_rows = []
    for b in range(n_seq):
        rows = slice(b * t_new, (b + 1) * t_new)
        h0 = h0_ref[b]
        ht = jnp.concatenate(
            [h0[j * LANES:(j + 1) * LANES, :].T for j in range(SSM_WIDTH // LANES)], axis=1)
        y_off = []
        new_states = []
        for g in range(N_SSM_GROUPS):
            gs = slice(g * D_STATE, (g + 1) * D_STATE)
            ws = slice(g * GROUP_WIDTH, (g + 1) * GROUP_WIDTH)
            y_off.append(_dot(cm[:, gs].astype(BF16), ht[:, ws].astype(BF16))[rows, :])
            bt = jnp.where(lane_seq == b, bm_t[g], 0.0)
            new_states.append(_dot(bt.astype(BF16), xw[:, ws].astype(BF16)))
        y_off_rows.append(jnp.concatenate(y_off, axis=1))
        last = (b + 1) * t_new - 1
        ht_new = exp_a[last:last + 1, :] * ht + jnp.concatenate(new_states, axis=1)
        for j in range(SSM_WIDTH // LANES):
            st_ref[b, j * LANES:(j + 1) * LANES, :] = ht_new[:, j * LANES:(j + 1) * LANES].T
    y_off_rows.append(jnp.zeros((CHUNK - n_rows, SSM_WIDTH), F32))
    y = y + jnp.concatenate(y_off_rows, axis=0) * exp_a

    y = y + drow_ref[...] * xs
    y = y * (z * _sigmoid(z))
    o_ref[...] = _rms(y, g_ref[...])[0:n_rows, :]


def _ssd_sample(proj, prev, h0, cw, cb, dtb, alog, valid, drow, g, e, row0, n_seq, t_new):
    tri, lsel = _ssd_consts(t_new)
    rb = row0 // CHUNK
    kern = functools.partial(_ssd_sample_kernel, n_seq=n_seq, t_new=t_new)
    full3 = pl.BlockSpec((n_seq, SSM_WIDTH, D_STATE), lambda i: (0, 0, 0))
    return pl.pallas_call(
        kern,
        grid=(1,),
        in_specs=[
            pl.BlockSpec((CHUNK, SSM_WIDTH), lambda i: (rb, COL_Z // SSM_WIDTH)),
            pl.BlockSpec((CHUNK, SSM_WIDTH), lambda i: (rb, COL_XS // SSM_WIDTH)),
            pl.BlockSpec((CHUNK, BC_WIDTH), lambda i: (rb, COL_BC // BC_WIDTH)),
            pl.BlockSpec((CHUNK, LANES), lambda i: (rb, COL_MISC // LANES)),
            _row_spec((CHUNK, CONV_DIM)),
            full3,
            _row_spec((CONV_W, CONV_DIM)), _row_spec((1, CONV_DIM)),
            _row_spec((1, LANES)), _row_spec((1, LANES)), _row_spec((1, LANES)),
            _row_spec((1, SSM_WIDTH)), _row_spec((1, SSM_WIDTH)),
            _row_spec((CHUNK, CHUNK)), _row_spec((CHUNK, CHUNK)), _row_spec((LANES, SSM_WIDTH)),
        ],
        out_specs=[
            pl.BlockSpec((n_seq * t_new, SSM_WIDTH), lambda i: (0, 0)),
            full3,
        ],
        out_shape=[
            jax.ShapeDtypeStruct((n_seq * t_new, SSM_WIDTH), F32),
            jax.ShapeDtypeStruct((n_seq, SSM_WIDTH, D_STATE), F32),
        ],
        compiler_params=_params("arbitrary"),
        name="ssd_sample",
    )(proj, proj, proj, proj, prev, h0, cw, cb, dtb, alog, valid, drow, g, tri, lsel, e)


def _out_proj_kernel(h_ref, ap_ref, sp_ref, as_ref, ss_ref, wa_ref, ws_ref, o_ref, *, n_prompt_tiles):
    i = pl.program_id(0)

    def mix(att, ssm):
        o_ref[...] = h_ref[...] + (_dot(att.astype(BF16), wa_ref[...]) + _dot(ssm.astype(BF16), ws_ref[...]))

    @pl.when(i < n_prompt_tiles)
    def _():
        mix(ap_ref[...], sp_ref[...])

    @pl.when(i >= n_prompt_tiles)
    def _():
        pad = jnp.zeros((h_ref.shape[0] - as_ref.shape[0], as_ref.shape[1]), F32)
        mix(jnp.concatenate([as_ref[...], pad], axis=0), jnp.concatenate([ss_ref[...], pad], axis=0))


def _out_proj(h, o_att_p, o_ssm_p, o_att_s, o_ssm_s, w):
    t = h.shape[0]
    tm = ROW_ALIGN
    n_prompt, n_sample = o_att_p.shape[0], o_att_s.shape[0]
    n_prompt_tiles = n_prompt // tm
    assert n_prompt % tm == 0 and t // tm == n_prompt_tiles + 1 and n_sample <= tm
    prompt_spec = pl.BlockSpec((tm, ATT_WIDTH), lambda i: (jnp.minimum(i, n_prompt_tiles - 1), 0))
    sample_spec = pl.BlockSpec((n_sample, ATT_WIDTH), lambda i: (0, 0))
    return pl.pallas_call(
        functools.partial(_out_proj_kernel, n_prompt_tiles=n_prompt_tiles),
        grid=(t // tm,),
        in_specs=[
            pl.BlockSpec((tm, D_MODEL), lambda i: (i, 0)),
            prompt_spec, prompt_spec, sample_spec, sample_spec,
            pl.BlockSpec((ATT_WIDTH, D_MODEL), lambda i: (0, 0)),
            pl.BlockSpec((SSM_WIDTH, D_MODEL), lambda i: (1, 0)),
        ],
        out_specs=pl.BlockSpec((tm, D_MODEL), lambda i: (i, 0)),
        out_shape=jax.ShapeDtypeStruct((t, D_MODEL), F32),
        compiler_params=_params("arbitrary"),
        name="out_proj",
    )(h, o_att_p, o_ssm_p, o_att_s, o_ssm_s, w, w)


def _mlp_kernel(h_ref, g_ref, wu_ref, wd_ref, o_ref, hn_ref):
    @pl.when(pl.program_id(1) == 0)
    def _():
        h = h_ref[...]
        hn_ref[...] = _rms(h, g_ref[...]).astype(BF16)
        o_ref[...] = h

    u = jnp.maximum(_dot(hn_ref[...], wu_ref[...]), 0.0)
    o_ref[...] += _dot((u * u).astype(BF16), wd_ref[...])


def _mlp(h, g, wu, wd):
    t = h.shape[0]
    tm = TOKEN_TILE
    tf = 512
    return pl.pallas_call(
        _mlp_kernel,
        grid=(t // tm, D_FF // tf),
        in_specs=[
            pl.BlockSpec((tm, D_MODEL), lambda i, f: (i, 0)),
            pl.BlockSpec((1, D_MODEL), lambda i, f: (0, 0)),
            pl.BlockSpec((D_MODEL, tf), lambda i, f: (0, f)),
            pl.BlockSpec((tf, D_MODEL), lambda i, f: (f, 0)),
        ],
        out_specs=pl.BlockSpec((tm, D_MODEL), lambda i, f: (i, 0)),
        out_shape=jax.ShapeDtypeStruct((t, D_MODEL), F32),
        scratch_shapes=[pltpu.VMEM((tm, D_MODEL), BF16)],
        compiler_params=_params("parallel", "arbitrary"),
        name="mlp",
    )(h, g, wu, wd)


def _ple_update(h_ref, p_ref, g_ref, wg_ref, wp_ref):
    h = h_ref[...]
    gate = _sigmoid(_dot(_rms(h, g_ref[...]).astype(BF16), wg_ref[...]))
    return h + _dot(p_ref[...].astype(BF16), wp_ref[...]) * gate


def _ple_kernel(h_ref, p_ref, g_ref, wg_ref, wp_ref, o_ref):
    o_ref[...] = _ple_update(h_ref, p_ref, g_ref, wg_ref, wp_ref)


def _ple_final_kernel(h_ref, p_ref, g_ref, wg_ref, wp_ref, gf_ref, yp_ref, ys_ref, *, n_prompt_tiles):
    y = _rms(_ple_update(h_ref, p_ref, g_ref, wg_ref, wp_ref), gf_ref[...])
    i = pl.program_id(0)

    @pl.when(i < n_prompt_tiles)
    def _():
        yp_ref[...] = y

    @pl.when(i >= n_prompt_tiles)
    def _():
        ys_ref[...] = y[0:ys_ref.shape[0], :]


def _ple(h, p, g, wg, wp, g_final=None, n_prompt=None, n_sample=None):
    t = h.shape[0]
    tm = ROW_ALIGN
    ple_dim = p.shape[1]
    row = lambda: pl.BlockSpec((1, D_MODEL), lambda i: (0, 0))
    in_specs = [
        pl.BlockSpec((tm, D_MODEL), lambda i: (i, 0)),
        pl.BlockSpec((tm, ple_dim), lambda i: (i, 0)),
        row(),
        pl.BlockSpec((D_MODEL, D_MODEL), lambda i: (0, 0)),
        pl.BlockSpec((ple_dim, D_MODEL), lambda i: (0, 0)),
    ]
    if g_final is None:
        return pl.pallas_call(
            _ple_kernel,
            grid=(t // tm,),
            in_specs=in_specs,
            out_specs=pl.BlockSpec((tm, D_MODEL), lambda i: (i, 0)),
            out_shape=jax.ShapeDtypeStruct((t, D_MODEL), F32),
            compiler_params=_params("parallel"),
            name="ple",
        )(h, p, g, wg, wp)
    n_prompt_tiles = n_prompt // tm
    assert n_prompt % tm == 0 and t // tm == n_prompt_tiles + 1 and n_sample <= tm
    return pl.pallas_call(
        functools.partial(_ple_final_kernel, n_prompt_tiles=n_prompt_tiles),
        grid=(t // tm,),
        in_specs=in_specs + [row()],
        out_specs=[
            pl.BlockSpec((tm, D_MODEL), lambda i: (jnp.minimum(i, n_prompt_tiles - 1), 0)),
            pl.BlockSpec((n_sample, D_MODEL), lambda i: (0, 0)),
        ],
        out_shape=[jax.ShapeDtypeStruct((n_prompt, D_MODEL), F32),
                   jax.ShapeDtypeStruct((n_sample, D_MODEL), F32)],
        compiler_params=_params("arbitrary"),
        name="ple_final",
    )(h, p, g, wg, wp, g_final)


def _misc_row(values, lane0):
    row = jnp.zeros((1, LANES), F32)
    return row.at[0, lane0:lane0 + values.shape[0]].set(values.astype(F32))


def kernel(x_prompt, x_sample, cache_k, cache_v, cache_logf, state_ssm, state_conv, page_table, p_prompt, p_sample, norm_mix, w_in, b_forget, conv_w, conv_b, dt_bias, a_log, d_skip, norm_attn_out, norm_ssm_out, w_out, norm_mlp, w_up, w_down, norm_ple, w_ple_gate, w_ple_proj, norm_final):
    batch, seq, _ = x_prompt.shape
    n_seq, t_new, _ = x_sample.shape
    depth = w_in.shape[0]
    n_prompt = batch * seq
    n_sample = n_seq * t_new
    t_pad = -(-(n_prompt + CHUNK) // TOKEN_TILE) * TOKEN_TILE
    n_tail = t_pad - n_prompt - n_sample
    n_phys, page = cache_k.shape[1], cache_k.shape[2]

    h = jnp.concatenate([x_prompt.reshape(n_prompt, D_MODEL), x_sample.reshape(n_sample, D_MODEL),
                         jnp.zeros((n_tail, D_MODEL), F32)], axis=0)
    ple_dim = p_prompt.shape[-1]
    p_all = jnp.concatenate([p_prompt.reshape(depth, n_prompt, ple_dim),
                             p_sample.reshape(depth, n_sample, ple_dim),
                             jnp.zeros((depth, n_tail, ple_dim), F32)], axis=1)
    o_f = 3 * ATT_WIDTH
    o_z = o_f + N_ATT_HEADS
    o_xbc = o_z + SSM_WIDTH
    o_dt = o_xbc + CONV_DIM
    w_in_r = jnp.concatenate(
        [w_in[:, :, :o_f], w_in[:, :, o_z:o_xbc], w_in[:, :, o_xbc:o_dt], w_in[:, :, o_f:o_z],
         w_in[:, :, o_dt:],
         jnp.zeros((depth, D_MODEL, PROJ_W - COL_MISC - N_ATT_HEADS - N_SSM_HEADS), F32)],
        axis=2).astype(BF16)
    w_out_b = w_out.astype(BF16)
    w_up_b = w_up.astype(BF16)
    w_down_b = w_down.astype(BF16)
    w_gate_b = w_ple_gate.astype(BF16)
    w_proj_b = w_ple_proj.astype(BF16)

    tri_full, _ = _ssd_consts(CHUNK)
    tri_seq, _ = _ssd_consts(t_new)
    tri2 = jnp.stack([tri_full, tri_seq])
    expand = _expand_matrix()
    valid_row = _misc_row(jnp.ones((N_SSM_HEADS,), F32), DT_LANE)
    pbias = _past_bias(cache_logf, page_table)
    causal_new = jnp.arange(t_new)[None, :] <= jnp.arange(t_new)[:, None]
    same_head = jnp.eye(N_ATT_HEADS, dtype=bool)

    outs = {name: [] for name in ("kp", "vp", "fp", "sp", "cp", "ks", "vs", "fs", "ss", "cs")}
    sample_rows = slice(n_prompt, n_prompt + n_sample)
    n_new = t_new * N_ATT_HEADS
    for l in range(depth):
        proj, k_rows, v_rows, k_new, v_new = _in_proj(h, norm_mix[l][None, :], w_in_r[l], n_prompt, n_sample)
        logf, cum, cumt = _gates(proj, _misc_row(b_forget[l], F_LANE), tri2, n_prompt, seq)

        o_att_p = _fox_prompt(proj, cum, cumt, norm_attn_out[l][None, :], batch, seq)
        qc = cumt[:, n_prompt:n_prompt + n_sample].reshape(N_ATT_HEADS, n_seq, t_new)
        qc = jnp.transpose(qc, (1, 0, 2))
        qc_col = qc.reshape(n_seq, N_ATT_HEADS * t_new, 1)
        bias_new = jnp.where(causal_new[None, None, :, :, None] & same_head[None, :, None, None, :],
                             -qc[:, :, None, :, None], NEG)
        bias_new = jnp.concatenate(
            [bias_new.reshape(n_seq, N_ATT_HEADS * t_new, t_new * N_ATT_HEADS),
             jnp.full((n_seq, N_ATT_HEADS * t_new, LANES - t_new * N_ATT_HEADS), NEG, F32)], axis=2)
        o_att_s = _fox_sample(proj, k_new.reshape(n_seq, n_new, ATT_HEAD_DIM),
                              v_new.reshape(n_seq, n_new, ATT_HEAD_DIM), qc_col, bias_new,
                              pbias, cache_k, cache_v, page_table, norm_attn_out[l][None, :], l,
                              n_prompt, t_new)

        dtb = _misc_row(dt_bias[l], DT_LANE)
        alog = _misc_row(a_log[l], DT_LANE)
        drow = jnp.repeat(d_skip[l].astype(F32), SSM_HEADDIM)[None, :]
        ssm_args = (conv_w[l], conv_b[l][None, :], dtb, alog, valid_row, drow,
                    norm_ssm_out[l][None, :], expand)
        o_ssm_p, st_p = _ssd_prompt(proj, *ssm_args, batch, seq)
        prev = jnp.concatenate(
            [jnp.zeros((n_seq, t_new - (CONV_W - 1), CONV_DIM), F32), state_conv[l]], axis=1)
        prev = jnp.concatenate([prev.reshape(n_sample, CONV_DIM),
                                jnp.zeros((CHUNK - n_sample, CONV_DIM), F32)], axis=0)
        h0 = state_ssm[l].reshape(n_seq, SSM_WIDTH, D_STATE)
        o_ssm_s, st_s = _ssd_sample(proj, prev, h0, *ssm_args, n_prompt, n_seq, t_new)

        h = _out_proj(h, o_att_p, o_ssm_p, o_att_s, o_ssm_s, w_out_b[l])
        h = _mlp(h, norm_mlp[l][None, :], w_up_b[l], w_down_b[l])
        ple_args = (h, p_all[l], norm_ple[l][None, :], w_gate_b[l], w_proj_b[l])
        if l < depth - 1:
            h = _ple(*ple_args)
        else:
            y_prompt, y_sample = _ple(*ple_args, norm_final[None, :], n_prompt, n_sample)

        xbc = slice(COL_XS, COL_XS + CONV_DIM)
        outs["kp"].append(k_rows[:n_prompt])
        outs["vp"].append(v_rows[:n_prompt])
        outs["ks"].append(k_new)
        outs["vs"].append(v_new)
        outs["fp"].append(logf[:n_prompt].reshape(batch, seq, N_ATT_HEADS))
        outs["sp"].append(st_p.reshape(batch, N_SSM_HEADS, SSM_HEADDIM, D_STATE))
        outs["cp"].append(jnp.stack(
            [proj[(b + 1) * seq - (CONV_W - 1):(b + 1) * seq, xbc] for b in range(batch)]))
        outs["fs"].append(logf[sample_rows].reshape(n_seq, t_new, N_ATT_HEADS))
        outs["ss"].append(st_s.reshape(n_seq, N_SSM_HEADS, SSM_HEADDIM, D_STATE))
        outs["cs"].append(jnp.stack(
            [proj[n_prompt + (b + 1) * t_new - (CONV_W - 1):n_prompt + (b + 1) * t_new, xbc]
             for b in range(n_seq)]))

    kv_shape_p = (depth, batch, seq, N_ATT_HEADS, ATT_HEAD_DIM)
    kv_shape_s = (depth, n_seq, t_new, N_ATT_HEADS, ATT_HEAD_DIM)
    stack = lambda name: jnp.stack(outs[name])
    return (y_prompt.reshape(batch, seq, D_MODEL), y_sample.reshape(n_seq, t_new, D_MODEL),
            stack("kp").reshape(kv_shape_p), stack("vp").reshape(kv_shape_p),
            stack("fp"), stack("sp"), stack("cp"),
            stack("ks").reshape(kv_shape_s), stack("vs").reshape(kv_shape_s),
            stack("fs"), stack("ss"), stack("cs"))
```

```python
import functools

import numpy as np
import jax
import jax.numpy as jnp
from jax import lax
from jax.experimental import pallas as pl
from jax.experimental.pallas import tpu as pltpu

F32 = jnp.float32
BF16 = jnp.bfloat16

D_MODEL = 2048
ATT_HEAD_DIM = 128
ATT_WIDTH = D_MODEL // 2
N_ATT_HEADS = ATT_WIDTH // ATT_HEAD_DIM
ATT_SCALE = ATT_HEAD_DIM ** -0.5
SSM_WIDTH = D_MODEL - ATT_WIDTH
SSM_HEADDIM = 64
N_SSM_HEADS = SSM_WIDTH // SSM_HEADDIM
N_SSM_GROUPS = 2
GROUP_WIDTH = SSM_WIDTH // N_SSM_GROUPS
D_STATE = 128
CONV_W = 4
BC_WIDTH = 2 * N_SSM_GROUPS * D_STATE
CONV_DIM = SSM_WIDTH + BC_WIDTH
D_FF = 4 * D_MODEL
RMS_EPS = 1e-6
NEG = -1e30

SUBLANES = 8
LANES = 128
VMEM_LIMIT = 56 * 1024 * 1024

COL_Q = 0
COL_K = ATT_WIDTH
COL_V = 2 * ATT_WIDTH
COL_Z = 3 * ATT_WIDTH
COL_XS = COL_Z + SSM_WIDTH
COL_BC = COL_XS + SSM_WIDTH
COL_MISC = COL_BC + BC_WIDTH
PROJ_TILE = ATT_WIDTH
PROJ_W = 6 * PROJ_TILE
F_LANE = 0
DT_LANE = N_ATT_HEADS

CHUNK = 128
TOKEN_TILE = 768
ROW_ALIGN = 256


def _rms(x, g):
    ms = jnp.mean(x * x, axis=-1, keepdims=True)
    return (x * lax.rsqrt(ms + RMS_EPS)) * g


def _sigmoid(x):
    return 1.0 / (1.0 + jnp.exp(-x))


def _softplus(x):
    return jnp.maximum(x, 0.0) + jnp.log1p(jnp.exp(-jnp.abs(x)))


def _dot(a, b):
    return jnp.dot(a, b, preferred_element_type=F32)


def _bf16_terms(x):
    hi = x.astype(BF16)
    rest = x - hi.astype(F32)
    mid = rest.astype(BF16)
    lo = (rest - mid.astype(F32)).astype(BF16)
    return hi, mid, lo


def _select_rows(sel, x):
    return functools.reduce(lambda a, b: a + b, [_dot(sel, t) for t in _bf16_terms(x)])


def _select_cols(x, sel):
    return functools.reduce(lambda a, b: a + b, [_dot(t, sel) for t in _bf16_terms(x)])


def _dot_nt(a, b):
    return lax.dot_general(a, b, (((1,), (1,)), ((), ())), preferred_element_type=F32)


def _tile_lanes(x, rep):
    return x if rep == 1 else jnp.concatenate([x] * rep, axis=1)


def _params(*semantics):
    return pltpu.CompilerParams(dimension_semantics=semantics, vmem_limit_bytes=VMEM_LIMIT)


IN_F = 3 * ATT_WIDTH
IN_Z = IN_F + N_ATT_HEADS
IN_DT = IN_Z + SSM_WIDTH + CONV_DIM
IN_PROJ_DIM = IN_DT + N_SSM_HEADS


def _reorder_w_in_kernel(w_ref, o_ref):
    w = w_ref[...]
    pad = jnp.zeros((w.shape[0], PROJ_W - COL_MISC - N_ATT_HEADS - N_SSM_HEADS), F32)
    o_ref[...] = jnp.concatenate(
        [w[:, :IN_F], w[:, IN_Z:IN_DT], w[:, IN_F:IN_Z], w[:, IN_DT:], pad], axis=1).astype(BF16)


def _reorder_w_in(w_in):
    depth = w_in.shape[0]
    assert w_in.shape[2] == IN_PROJ_DIM
    rows = ROW_ALIGN
    return pl.pallas_call(
        _reorder_w_in_kernel,
        grid=(depth, D_MODEL // rows),
        in_specs=[pl.BlockSpec((None, rows, IN_PROJ_DIM), lambda l, i: (l, i, 0))],
        out_specs=pl.BlockSpec((None, rows, PROJ_W), lambda l, i: (l, i, 0)),
        out_shape=jax.ShapeDtypeStruct((depth, D_MODEL, PROJ_W), BF16),
        compiler_params=_params("parallel", "parallel"),
        name="w_in_reorder",
    )(w_in)


def _in_proj_kernel(x_ref, g_ref, w_ref, *rest, n_prompt, n_sample):
    o_ref, kp_ref, vp_ref, ks_ref, vs_ref, xn_ref = rest
    i = pl.program_id(0)
    j = pl.program_id(1)
    tm = x_ref.shape[0]

    @pl.when(j == 0)
    def _():
        xn_ref[...] = _rms(x_ref[...], g_ref[...]).astype(BF16)

    res = _dot(xn_ref[...], w_ref[...])
    o_ref[...] = res

    def emit(prompt_ref, sample_ref):
        prompt_ref[...] = res.reshape(tm, N_ATT_HEADS, ATT_HEAD_DIM)

        @pl.when(i == n_prompt // tm)
        def _():
            r0 = n_prompt % tm
            sample_ref[...] = res[r0:r0 + n_sample, :].reshape(n_sample, N_ATT_HEADS, ATT_HEAD_DIM)

    @pl.when(j == COL_K // PROJ_TILE)
    def _():
        emit(kp_ref, ks_ref)

    @pl.when(j == COL_V // PROJ_TILE)
    def _():
        emit(vp_ref, vs_ref)


def _in_proj(h, g, w, layer, n_prompt, n_sample):
    t = h.shape[0]
    tm = TOKEN_TILE
    assert n_prompt % tm + n_sample <= tm
    head_shape = (N_ATT_HEADS, ATT_HEAD_DIM)
    cache_shapes = [(t,) + head_shape] * 2 + [(n_sample,) + head_shape] * 2
    cache_specs = [pl.BlockSpec((tm,) + head_shape, lambda i, j: (i, 0, 0))] * 2 + \
                  [pl.BlockSpec((n_sample,) + head_shape, lambda i, j: (0, 0, 0))] * 2
    return pl.pallas_call(
        functools.partial(_in_proj_kernel, n_prompt=n_prompt, n_sample=n_sample),
        grid=(t // tm, PROJ_W // PROJ_TILE),
        in_specs=[
            pl.BlockSpec((tm, D_MODEL), lambda i, j: (i, 0)),
            pl.BlockSpec((1, D_MODEL), lambda i, j: (0, 0)),
            pl.BlockSpec((None, D_MODEL, PROJ_TILE), lambda i, j: (layer, 0, j)),
        ],
        out_specs=[pl.BlockSpec((tm, PROJ_TILE), lambda i, j: (i, j))] + cache_specs,
        out_shape=[jax.ShapeDtypeStruct((t, PROJ_W), F32)]
                  + [jax.ShapeDtypeStruct(s, F32) for s in cache_shapes],
        scratch_shapes=[pltpu.VMEM((tm, D_MODEL), BF16)],
        compiler_params=_params("arbitrary", "arbitrary"),
        name="in_proj",
    )(h, g, w)


def _gates_kernel(misc_ref, fb_ref, tri_ref, logf_ref, cum_ref, cumt_ref, carry_ref, *,
                  chunks_per_seq, n_prompt_chunks):
    i = pl.program_id(0)

    @pl.when(jnp.logical_or(lax.rem(i, chunks_per_seq) == 0, i >= n_prompt_chunks))
    def _():
        carry_ref[...] = jnp.zeros_like(carry_ref)

    x = misc_ref[...] + fb_ref[...]
    lf = jnp.minimum(x, 0.0) - jnp.log1p(jnp.exp(-jnp.abs(x)))
    c = _select_rows(tri_ref[...], lf) + carry_ref[...]
    carry_ref[...] = c[CHUNK - 1:CHUNK, :]
    logf_ref[...] = lf[:, F_LANE:F_LANE + N_ATT_HEADS]
    cum_ref[...] = c[:, F_LANE:F_LANE + N_ATT_HEADS]
    cumt_ref[...] = c.T[F_LANE:F_LANE + N_ATT_HEADS, :]


def _gates(proj, fb_row, tri2, n_prompt_rows, seq):
    n_prompt_chunks = n_prompt_rows // CHUNK
    n_chunks = n_prompt_chunks + 1
    rows = n_chunks * CHUNK
    kern = functools.partial(_gates_kernel, chunks_per_seq=seq // CHUNK,
                             n_prompt_chunks=n_prompt_chunks)
    return pl.pallas_call(
        kern,
        grid=(n_chunks,),
        in_specs=[
            pl.BlockSpec((CHUNK, LANES), lambda i: (i, COL_MISC // LANES)),
            pl.BlockSpec((1, LANES), lambda i: (0, 0)),
            pl.BlockSpec((None, CHUNK, CHUNK), lambda i: (i // n_prompt_chunks, 0, 0)),
        ],
        out_specs=[
            pl.BlockSpec((CHUNK, N_ATT_HEADS), lambda i: (i, 0)),
            pl.BlockSpec((CHUNK, N_ATT_HEADS), lambda i: (i, 0)),
            pl.BlockSpec((N_ATT_HEADS, CHUNK), lambda i: (0, i)),
        ],
        out_shape=[
            jax.ShapeDtypeStruct((rows, N_ATT_HEADS), F32),
            jax.ShapeDtypeStruct((rows, N_ATT_HEADS), F32),
            jax.ShapeDtypeStruct((N_ATT_HEADS, rows), F32),
        ],
        scratch_shapes=[pltpu.VMEM((1, LANES), F32)],
        compiler_params=_params("arbitrary"),
        name="gates",
    )(proj, fb_row, tri2)


def _fox_prompt_kernel(qi_ref, ki_ref, q_ref, k_ref, v_ref, cq_ref, ck_ref, g_ref, o_ref,
                       m_ref, l_ref, acc_ref, cqb_ref, *, tq, tk):
    p = pl.program_id(1)
    qi = qi_ref[p]
    ki = ki_ref[p]
    rep = tk // LANES

    @pl.when(ki == 0)
    def _():
        m_ref[...] = jnp.full_like(m_ref, NEG)
        l_ref[...] = jnp.zeros_like(l_ref)
        acc_ref[...] = jnp.zeros_like(acc_ref)
        cq = cq_ref[...]
        for h in range(N_ATT_HEADS):
            cqb_ref[h] = jnp.broadcast_to(cq[:, h:h + 1], (tq, LANES))

    def step(on_diagonal):
        ck = ck_ref[...]
        if on_diagonal:
            visible = (lax.broadcasted_iota(jnp.int32, (tq, tk), 1)
                       <= lax.broadcasted_iota(jnp.int32, (tq, tk), 0))
        def scores(h):
            sl = slice(h * ATT_HEAD_DIM, (h + 1) * ATT_HEAD_DIM)
            return _dot_nt(q_ref[:, sl].astype(BF16), k_ref[:, sl].astype(BF16))

        s_next = scores(0)
        for h in range(N_ATT_HEADS):
            sl = slice(h * ATT_HEAD_DIM, (h + 1) * ATT_HEAD_DIM)
            s = s_next
            if h + 1 < N_ATT_HEADS:
                s_next = scores(h + 1)
            s = s * ATT_SCALE + (_tile_lanes(cqb_ref[h], rep) - ck[h:h + 1, :])
            if on_diagonal:
                s = jnp.where(visible, s, NEG)
            m_prev = m_ref[h]
            m_new = jnp.maximum(m_prev, jnp.max(s, axis=-1, keepdims=True))
            alpha = jnp.exp(m_prev - m_new)
            pm = jnp.exp(s - _tile_lanes(m_new, rep))
            l_ref[h] = alpha * l_ref[h] + jnp.sum(pm, axis=-1, keepdims=True)
            acc_ref[:, sl] = alpha * acc_ref[:, sl] + _dot(pm.astype(BF16), v_ref[:, sl].astype(BF16))
            m_ref[h] = m_new

    @pl.when(ki < qi)
    def _():
        step(False)

    @pl.when(ki == qi)
    def _():
        step(True)
        for h in range(N_ATT_HEADS):
            sl = slice(h * ATT_HEAD_DIM, (h + 1) * ATT_HEAD_DIM)
            acc_ref[:, sl] = acc_ref[:, sl] / l_ref[h]
        o_ref[...] = _rms(acc_ref[...], g_ref[...])


def _fox_prompt(proj, cum, cumt, g, batch, seq):
    tq = tk = min(512, seq)
    nq = seq // tq
    pairs = [(a, b) for a in range(nq) for b in range(a + 1)]
    qi_tab = jnp.asarray(np.array([a for a, _ in pairs], np.int32))
    ki_tab = jnp.asarray(np.array([b for _, b in pairs], np.int32))
    kern = functools.partial(_fox_prompt_kernel, tq=tq, tk=tk)
    grid_spec = pltpu.PrefetchScalarGridSpec(
        num_scalar_prefetch=2,
        grid=(batch, len(pairs)),
        in_specs=[
            pl.BlockSpec((tq, ATT_WIDTH), lambda b, p, qt, kt: (b * nq + qt[p], COL_Q // ATT_WIDTH)),
            pl.BlockSpec((tk, ATT_WIDTH), lambda b, p, qt, kt: (b * nq + kt[p], COL_K // ATT_WIDTH)),
            pl.BlockSpec((tk, ATT_WIDTH), lambda b, p, qt, kt: (b * nq + kt[p], COL_V // ATT_WIDTH)),
            pl.BlockSpec((tq, N_ATT_HEADS), lambda b, p, qt, kt: (b * nq + qt[p], 0)),
            pl.BlockSpec((N_ATT_HEADS, tk), lambda b, p, qt, kt: (0, b * nq + kt[p])),
            pl.BlockSpec((1, ATT_WIDTH), lambda b, p, qt, kt: (0, 0)),
        ],
        out_specs=pl.BlockSpec((tq, ATT_WIDTH), lambda b, p, qt, kt: (b * nq + qt[p], 0)),
        scratch_shapes=[
            pltpu.VMEM((N_ATT_HEADS, tq, LANES), F32),
            pltpu.VMEM((N_ATT_HEADS, tq, LANES), F32),
            pltpu.VMEM((tq, ATT_WIDTH), F32),
            pltpu.VMEM((N_ATT_HEADS, tq, LANES), F32),
        ],
    )
    return pl.pallas_call(
        kern,
        grid_spec=grid_spec,
        out_shape=jax.ShapeDtypeStruct((batch * seq, ATT_WIDTH), F32),
        compiler_params=_params("parallel", "arbitrary"),
        name="fox_prompt",
    )(qi_tab, ki_tab, proj, proj, proj, cum, cumt, g)


def _past_bias_kernel(pt_ref, logf_hbm, m1_ref, s_ref, su_ref, e8_ref, o_ref, x_ref, sem, *,
                      n_pages):
    l = pl.program_id(0)
    b = pl.program_id(1)

    def page_copy(p):
        page = pt_ref[b * n_pages + p]
        return pltpu.make_async_copy(logf_hbm.at[l, pl.ds(page, 1), :], x_ref.at[pl.ds(p, 1), :], sem)

    def start(p, carry):
        page_copy(p).start()
        return carry

    def wait(p, carry):
        page_copy(p).wait()
        return carry

    lax.fori_loop(0, n_pages, start, 0)
    lax.fori_loop(0, n_pages, wait, 0)
    x = x_ref[...]
    within = _select_cols(x, m1_ref[...])
    totals = _select_cols(x, s_ref[...])
    later = _select_rows(su_ref[...], totals)
    o_ref[...] = within + _select_cols(later, e8_ref[...])


def _past_bias(cache_logf, page_table):
    depth, n_phys, page, heads = cache_logf.shape
    n_seq, n_pages = page_table.shape
    width = page * heads
    r = lax.broadcasted_iota(jnp.int32, (width, width), 0)
    c = lax.broadcasted_iota(jnp.int32, (width, width), 1)
    m1 = ((r % heads == c % heads) & (r // heads > c // heads)).astype(BF16)
    r = lax.broadcasted_iota(jnp.int32, (width, LANES), 0)
    c = lax.broadcasted_iota(jnp.int32, (width, LANES), 1)
    s = (r % heads == c).astype(BF16)
    r = lax.broadcasted_iota(jnp.int32, (n_pages, n_pages), 0)
    c = lax.broadcasted_iota(jnp.int32, (n_pages, n_pages), 1)
    su = (c > r).astype(BF16)
    r = lax.broadcasted_iota(jnp.int32, (LANES, width), 0)
    c = lax.broadcasted_iota(jnp.int32, (LANES, width), 1)
    e8 = (r == c % heads).astype(BF16)
    const = lambda shape: pl.BlockSpec(shape, lambda l, b, pt: (0, 0))
    grid_spec = pltpu.PrefetchScalarGridSpec(
        num_scalar_prefetch=1,
        grid=(depth, n_seq),
        in_specs=[
            pl.BlockSpec(memory_space=pl.ANY),
            const((width, width)), const((width, LANES)), const((n_pages, n_pages)),
            const((LANES, width)),
        ],
        out_specs=pl.BlockSpec((None, None, n_pages, width), lambda l, b, pt: (l, b, 0, 0)),
        scratch_shapes=[pltpu.VMEM((n_pages, width), F32), pltpu.SemaphoreType.DMA(())],
    )
    out = pl.pallas_call(
        functools.partial(_past_bias_kernel, n_pages=n_pages),
        grid_spec=grid_spec,
        out_shape=jax.ShapeDtypeStruct((depth, n_seq, n_pages, width), F32),
        compiler_params=_params("arbitrary", "arbitrary"),
        name="past_bias",
    )(page_table.reshape(-1), cache_logf.reshape(depth, n_phys, width), m1, s, su, e8)
    return out


def _fox_sample_kernel(pt_ref, q_ref, kn_ref, vn_ref, qc_ref, bn_ref, pb_ref, g_ref, *rest,
                       pages_per_step, t_new):
    k_refs = rest[:pages_per_step]
    v_refs = rest[pages_per_step:2 * pages_per_step]
    o_ref, qall_ref, qcm_ref, kpad_ref, vpad_ref, m_ref, l_ref, acc_ref = rest[2 * pages_per_step:]
    p = pl.program_id(1)
    rows, width = qcm_ref.shape

    @pl.when(p == 0)
    def _():
        m_ref[...] = jnp.full_like(m_ref, NEG)
        l_ref[...] = jnp.zeros_like(l_ref)
        acc_ref[...] = jnp.zeros_like(acc_ref)
        for h in range(N_ATT_HEADS):
            qall_ref[h * t_new:(h + 1) * t_new, :] = q_ref[:, h * ATT_HEAD_DIM:(h + 1) * ATT_HEAD_DIM]
        head_of_row = lax.broadcasted_iota(jnp.int32, (rows, width), 0) // t_new
        head_of_lane = lax.rem(lax.broadcasted_iota(jnp.int32, (rows, width), 1), N_ATT_HEADS)
        qcm_ref[...] = qc_ref[...] + jnp.where(head_of_row == head_of_lane, 0.0, NEG)

    qall = qall_ref[...].astype(BF16)

    def attend(pages):
        scores = [_dot_nt(qall, k_flat.astype(BF16)) * ATT_SCALE + bias for k_flat, _, bias in pages]
        m_cur = functools.reduce(jnp.maximum, [jnp.max(s, axis=-1, keepdims=True) for s in scores])
        m_prev = m_ref[...]
        m_new = jnp.maximum(m_prev, m_cur)
        alpha = jnp.exp(m_prev - m_new)
        l_new = alpha * l_ref[...]
        acc_new = alpha * acc_ref[...]
        for s, (_, v_flat, _) in zip(scores, pages):
            pm = jnp.exp(s - _tile_lanes(m_new, s.shape[1] // LANES))
            l_new = l_new + jnp.sum(pm, axis=-1, keepdims=True)
            acc_new = acc_new + _dot(pm.astype(BF16), v_flat.astype(BF16))
        l_ref[...] = l_new
        acc_ref[...] = acc_new
        m_ref[...] = m_new

    qcm = qcm_ref[...]
    attend([(k_refs[i][...].reshape(width, ATT_HEAD_DIM), v_refs[i][...].reshape(width, ATT_HEAD_DIM),
             qcm + pb_ref[i:i + 1, :]) for i in range(pages_per_step)])

    @pl.when(p == pl.num_programs(1) - 1)
    def _():
        n_new = kn_ref.shape[0]
        kpad_ref[...] = jnp.zeros_like(kpad_ref)
        vpad_ref[...] = jnp.zeros_like(vpad_ref)
        kpad_ref[0:n_new, :] = kn_ref[...]
        vpad_ref[0:n_new, :] = vn_ref[...]
        attend([(kpad_ref[...], vpad_ref[...], qc_ref[...] + bn_ref[...])])
        o_full = acc_ref[...] / l_ref[...]
        o = jnp.concatenate([o_full[h * t_new:(h + 1) * t_new, :] for h in range(N_ATT_HEADS)], axis=1)
        o_ref[...] = _rms(o, g_ref[...])


def _fox_sample(proj, k_new, v_new, qc_col, bias_new, pbias, cache_k, cache_v, page_table, g, layer,
                row0, t_new):
    n_seq, n_pages = page_table.shape
    page = cache_k.shape[2]
    width = page * N_ATT_HEADS
    pages_per_step = 8
    steps = n_pages // pages_per_step
    rows = N_ATT_HEADS * t_new
    n_new = t_new * N_ATT_HEADS
    rb0 = row0 // t_new

    def page_spec(i):
        return pl.BlockSpec(
            (None, None, page, N_ATT_HEADS, ATT_HEAD_DIM),
            lambda b, p, pt: (layer, pt[b * n_pages + p * pages_per_step + i], 0, 0, 0))

    per_seq = lambda shape: pl.BlockSpec((None,) + shape, lambda b, p, pt: (b, 0, 0))
    new_rows = lambda: per_seq((n_new, ATT_HEAD_DIM))
    grid_spec = pltpu.PrefetchScalarGridSpec(
        num_scalar_prefetch=1,
        grid=(n_seq, steps),
        in_specs=[
            pl.BlockSpec((t_new, ATT_WIDTH), lambda b, p, pt: (rb0 + b, COL_Q // ATT_WIDTH)),
            new_rows(), new_rows(),
            per_seq((rows, 1)), per_seq((rows, LANES)),
            pl.BlockSpec((None, None, pages_per_step, width), lambda b, p, pt: (layer, b, p, 0)),
            pl.BlockSpec((1, ATT_WIDTH), lambda b, p, pt: (0, 0)),
        ] + [page_spec(i) for i in range(pages_per_step)] * 2,
        out_specs=pl.BlockSpec((t_new, ATT_WIDTH), lambda b, p, pt: (b, 0)),
        scratch_shapes=[
            pltpu.VMEM((rows, ATT_HEAD_DIM), F32),
            pltpu.VMEM((rows, width), F32),
            pltpu.VMEM((LANES, ATT_HEAD_DIM), F32),
            pltpu.VMEM((LANES, ATT_HEAD_DIM), F32),
            pltpu.VMEM((rows, LANES), F32),
            pltpu.VMEM((rows, LANES), F32),
            pltpu.VMEM((rows, ATT_HEAD_DIM), F32),
        ],
    )
    kern = functools.partial(_fox_sample_kernel, pages_per_step=pages_per_step, t_new=t_new)
    return pl.pallas_call(
        kern,
        grid_spec=grid_spec,
        out_shape=jax.ShapeDtypeStruct((n_seq * t_new, ATT_WIDTH), F32),
        compiler_params=_params("parallel", "arbitrary"),
        name="fox_sample",
    )(page_table.reshape(-1), proj, k_new, v_new, qc_col, bias_new, pbias, g,
      *([cache_k] * pages_per_step), *([cache_v] * pages_per_step))


def _causal_conv(x, prev, w, b, rows_per_seq):
    n = x.shape[0]
    row = lax.broadcasted_iota(jnp.int32, x.shape, 0)
    out = None
    for j in range(CONV_W - 1, 0, -1):
        if rows_per_seq is None:
            shifted = jnp.where(row >= j, pltpu.roll(x, j, 0), pltpu.roll(prev, j, 0))
        else:
            shifted = jnp.where(lax.rem(row, rows_per_seq) >= j, pltpu.roll(x, j, 0),
                                pltpu.roll(prev, (j - rows_per_seq) % n, 0))
        tap = CONV_W - 1 - j
        term = shifted * w[tap:tap + 1, :]
        out = term if out is None else out + term
    out = out + x * w[CONV_W - 1:CONV_W, :]
    out = out + b
    return out * _sigmoid(out)


def _ssd_scalars(misc, dtb, alog, valid, tri, lsel):
    dtf = _softplus(misc + dtb)
    a = dtf * (-(jnp.exp(alog) * valid))
    acum = _select_rows(tri, a)
    if lsel is None:
        acum_end = acum[CHUNK - 1:CHUNK, :]
    else:
        acum_end = _select_rows(lsel, acum)
    return dtf, acum, acum_end


def _ssd_diag(xs, bm, cm, dtf, acum, visible):
    acum_t = acum.T
    dt_t = dtf.T
    lane = lax.broadcasted_iota(jnp.int32, (xs.shape[0], LANES), 1)
    heads_per_group = N_SSM_HEADS // N_SSM_GROUPS
    blocks = []
    for g in range(N_SSM_GROUPS):
        gs = slice(g * D_STATE, (g + 1) * D_STATE)
        cb = _dot_nt(cm[:, gs].astype(BF16), bm[:, gs].astype(BF16))
        for j in range(heads_per_group // 2):
            blk = g * (heads_per_group // 2) + j
            xpair = xs[:, blk * LANES:(blk + 1) * LANES]
            y = None
            for hh in range(2):
                ch = DT_LANE + 2 * blk + hh
                seg = acum[:, ch:ch + 1] - acum_t[ch:ch + 1, :]
                decay = jnp.exp(jnp.where(visible, seg, NEG))
                m = (cb * decay) * dt_t[ch:ch + 1, :]
                in_half = (lane >= SSM_HEADDIM) if hh else (lane < SSM_HEADDIM)
                xh = jnp.where(in_half, xpair, 0.0)
                part = _dot(m.astype(BF16), xh.astype(BF16))
                y = part if y is None else y + part
            blocks.append(y)
    return jnp.concatenate(blocks, axis=1)


def _ssd_prompt_kernel(z_ref, xs_ref, bc_ref, misc_ref, cw_ref, cb_ref, dtb_ref, alog_ref,
                       valid_ref, drow_ref, g_ref, tri_ref, e_ref,
                       o_ref, st_ref, ht_ref, pxs_ref, pbc_ref, *, chunks_per_seq):
    c = lax.rem(pl.program_id(0), chunks_per_seq)

    @pl.when(c == 0)
    def _():
        ht_ref[...] = jnp.zeros_like(ht_ref)
        pxs_ref[...] = jnp.zeros_like(pxs_ref)
        pbc_ref[...] = jnp.zeros_like(pbc_ref)

    xs_raw = xs_ref[...]
    bc_raw = bc_ref[...]
    cw = cw_ref[...]
    cb = cb_ref[...]
    xs = _causal_conv(xs_raw, pxs_ref[...], cw[:, :SSM_WIDTH], cb[:, :SSM_WIDTH], None)
    bc = _causal_conv(bc_raw, pbc_ref[...], cw[:, SSM_WIDTH:], cb[:, SSM_WIDTH:], None)
    pxs_ref[...] = xs_raw
    pbc_ref[...] = bc_raw
    bm = bc[:, :N_SSM_GROUPS * D_STATE]
    cm = bc[:, N_SSM_GROUPS * D_STATE:]

    dtf, acum, acum_end = _ssd_scalars(misc_ref[...], dtb_ref[...], alog_ref[...], valid_ref[...],
                                       tri_ref[...], None)
    row = lax.broadcasted_iota(jnp.int32, (CHUNK, CHUNK), 0)
    col = lax.broadcasted_iota(jnp.int32, (CHUNK, CHUNK), 1)
    y = _ssd_diag(xs, bm, cm, dtf, acum, col <= row)

    e = e_ref[...]
    exp_a = jnp.exp(_select_cols(acum, e))
    w_end = _select_cols(jnp.exp(acum_end - acum) * dtf, e)
    xw = xs * w_end
    ht = ht_ref[...]
    y_off = []
    new_states = []
    for g in range(N_SSM_GROUPS):
        gs = slice(g * D_STATE, (g + 1) * D_STATE)
        ws = slice(g * GROUP_WIDTH, (g + 1) * GROUP_WIDTH)
        y_off.append(_dot(cm[:, gs].astype(BF16), ht[:, ws].astype(BF16)))
        new_states.append(_dot(bm[:, gs].T.astype(BF16), xw[:, ws].astype(BF16)))
    y = y + jnp.concatenate(y_off, axis=1) * exp_a
    ht_new = exp_a[CHUNK - 1:CHUNK, :] * ht + jnp.concatenate(new_states, axis=1)
    ht_ref[...] = ht_new

    y = y + drow_ref[...] * xs
    z = z_ref[...]
    y = y * (z * _sigmoid(z))
    o_ref[...] = _rms(y, g_ref[...])

    @pl.when(c == chunks_per_seq - 1)
    def _():
        for j in range(SSM_WIDTH // LANES):
            st_ref[j * LANES:(j + 1) * LANES, :] = ht_new[:, j * LANES:(j + 1) * LANES].T


def _ssd_consts(l_rows_per_seq):
    r = lax.broadcasted_iota(jnp.int32, (CHUNK, CHUNK), 0)
    c = lax.broadcasted_iota(jnp.int32, (CHUNK, CHUNK), 1)
    same = (r // l_rows_per_seq) == (c // l_rows_per_seq)
    tri = (same & (c <= r)).astype(BF16)
    lsel = (c == (r // l_rows_per_seq) * l_rows_per_seq + l_rows_per_seq - 1).astype(BF16)
    return tri, lsel


def _expand_matrix():
    r = lax.broadcasted_iota(jnp.int32, (LANES, SSM_WIDTH), 0)
    c = lax.broadcasted_iota(jnp.int32, (LANES, SSM_WIDTH), 1)
    return (r == DT_LANE + c // SSM_HEADDIM).astype(BF16)


def _row_spec(shape):
    return pl.BlockSpec(shape, lambda i: (0, 0))


def _ssd_prompt(proj, cw, cb, dtb, alog, valid, drow, g, e, batch, seq):
    chunks_per_seq = seq // CHUNK
    tri, _ = _ssd_consts(CHUNK)
    kern = functools.partial(_ssd_prompt_kernel, chunks_per_seq=chunks_per_seq)
    return pl.pallas_call(
        kern,
        grid=(batch * chunks_per_seq,),
        in_specs=[
            pl.BlockSpec((CHUNK, SSM_WIDTH), lambda i: (i, COL_Z // SSM_WIDTH)),
            pl.BlockSpec((CHUNK, SSM_WIDTH), lambda i: (i, COL_XS // SSM_WIDTH)),
            pl.BlockSpec((CHUNK, BC_WIDTH), lambda i: (i, COL_BC // BC_WIDTH)),
            pl.BlockSpec((CHUNK, LANES), lambda i: (i, COL_MISC // LANES)),
            _row_spec((CONV_W, CONV_DIM)), _row_spec((1, CONV_DIM)),
            _row_spec((1, LANES)), _row_spec((1, LANES)), _row_spec((1, LANES)),
            _row_spec((1, SSM_WIDTH)), _row_spec((1, SSM_WIDTH)),
            _row_spec((CHUNK, CHUNK)), _row_spec((LANES, SSM_WIDTH)),
        ],
        out_specs=[
            pl.BlockSpec((CHUNK, SSM_WIDTH), lambda i: (i, 0)),
            pl.BlockSpec((None, SSM_WIDTH, D_STATE), lambda i: (i // chunks_per_seq, 0, 0)),
        ],
        out_shape=[
            jax.ShapeDtypeStruct((batch * seq, SSM_WIDTH), F32),
            jax.ShapeDtypeStruct((batch, SSM_WIDTH, D_STATE), F32),
        ],
        scratch_shapes=[
            pltpu.VMEM((D_STATE, SSM_WIDTH), F32),
            pltpu.VMEM((CHUNK, SSM_WIDTH), F32),
            pltpu.VMEM((CHUNK, BC_WIDTH), F32),
        ],
        compiler_params=_params("arbitrary"),
        name="ssd_prompt",
    )(proj, proj, proj, proj, cw, cb, dtb, alog, valid, drow, g, tri, e)


def _ssd_sample_kernel(z_ref, xs_ref, bc_ref, misc_ref, prev_ref, h0_ref, cw_ref, cb_ref, dtb_ref,
                       alog_ref, valid_ref, drow_ref, g_ref, tri_ref, lsel_ref, e_ref,
                       o_ref, st_ref, *, n_seq, t_new):
    n_rows = n_seq * t_new
    live = lax.broadcasted_iota(jnp.int32, (CHUNK, 1), 0) < n_rows
    xs_raw = jnp.where(live, xs_ref[...], 0.0)
    bc_raw = jnp.where(live, bc_ref[...], 0.0)
    z = jnp.where(live, z_ref[...], 0.0)
    misc = jnp.where(live, misc_ref[...], 0.0)
    cw = cw_ref[...]
    cb = cb_ref[...]
    prev = prev_ref[...]
    xs = _causal_conv(xs_raw, prev[:, :SSM_WIDTH], cw[:, :SSM_WIDTH], cb[:, :SSM_WIDTH], t_new)
    bc = _causal_conv(bc_raw, prev[:, SSM_WIDTH:], cw[:, SSM_WIDTH:], cb[:, SSM_WIDTH:], t_new)
    bm = bc[:, :N_SSM_GROUPS * D_STATE]
    cm = bc[:, N_SSM_GROUPS * D_STATE:]

    dtf, acum, acum_end = _ssd_scalars(misc, dtb_ref[...], alog_ref[...], valid_ref[...],
                                       tri_ref[...], lsel_ref[...])
    row = lax.broadcasted_iota(jnp.int32, (CHUNK, CHUNK), 0)
    col = lax.broadcasted_iota(jnp.int32, (CHUNK, CHUNK), 1)
    visible = (row // t_new == col // t_new) & (col <= row)
    y = _ssd_diag(xs, bm, cm, dtf, acum, visible)

    e = e_ref[...]
    exp_a = jnp.exp(_select_cols(acum, e))
    w_end = _select_cols(jnp.exp(acum_end - acum) * dtf, e)
    xw = xs * w_end
    bm_t = [bm[:, g * D_STATE:(g + 1) * D_STATE].T for g in range(N_SSM_GROUPS)]
    lane_seq = lax.broadcasted_iota(jnp.int32, (D_STATE, CHUNK), 1) // t_new

    y_off_rows = []
    for b in range(n_seq):
        rows = slice(b * t_new, (b + 1) * t_new)
        h0 = h0_ref[b]
        ht = jnp.concatenate(
            [h0[j * LANES:(j + 1) * LANES, :].T for j in range(SSM_WIDTH // LANES)], axis=1)
        y_off = []
        new_states = []
        for g in range(N_SSM_GROUPS):
            gs = slice(g * D_STATE, (g + 1) * D_STATE)
            ws = slice(g * GROUP_WIDTH, (g + 1) * GROUP_WIDTH)
            y_off.append(_dot(cm[:, gs].astype(BF16), ht[:, ws].astype(BF16))[rows, :])
            bt = jnp.where(lane_seq == b, bm_t[g], 0.0)
            new_states.append(_dot(bt.astype(BF16), xw[:, ws].astype(BF16)))
        y_off_rows.append(jnp.concatenate(y_off, axis=1))
        last = (b + 1) * t_new - 1
        ht_new = exp_a[last:last + 1, :] * ht + jnp.concatenate(new_states, axis=1)
        for j in range(SSM_WIDTH // LANES):
            st_ref[b, j * LANES:(j + 1) * LANES, :] = ht_new[:, j * LANES:(j + 1) * LANES].T
    y_off_rows.append(jnp.zeros((CHUNK - n_rows, SSM_WIDTH), F32))
    y = y + jnp.concatenate(y_off_rows, axis=0) * exp_a

    y = y + drow_ref[...] * xs
    y = y * (z * _sigmoid(z))
    o_ref[...] = _rms(y, g_ref[...])[0:n_rows, :]


def _ssd_sample(proj, prev, h0, cw, cb, dtb, alog, valid, drow, g, e, row0, n_seq, t_new):
    tri, lsel = _ssd_consts(t_new)
    rb = row0 // CHUNK
    kern = functools.partial(_ssd_sample_kernel, n_seq=n_seq, t_new=t_new)
    full3 = pl.BlockSpec((n_seq, SSM_WIDTH, D_STATE), lambda i: (0, 0, 0))
    return pl.pallas_call(
        kern,
        grid=(1,),
        in_specs=[
            pl.BlockSpec((CHUNK, SSM_WIDTH), lambda i: (rb, COL_Z // SSM_WIDTH)),
            pl.BlockSpec((CHUNK, SSM_WIDTH), lambda i: (rb, COL_XS // SSM_WIDTH)),
            pl.BlockSpec((CHUNK, BC_WIDTH), lambda i: (rb, COL_BC // BC_WIDTH)),
            pl.BlockSpec((CHUNK, LANES), lambda i: (rb, COL_MISC // LANES)),
            _row_spec((CHUNK, CONV_DIM)),
            full3,
            _row_spec((CONV_W, CONV_DIM)), _row_spec((1, CONV_DIM)),
            _row_spec((1, LANES)), _row_spec((1, LANES)), _row_spec((1, LANES)),
            _row_spec((1, SSM_WIDTH)), _row_spec((1, SSM_WIDTH)),
            _row_spec((CHUNK, CHUNK)), _row_spec((CHUNK, CHUNK)), _row_spec((LANES, SSM_WIDTH)),
        ],
        out_specs=[
            pl.BlockSpec((n_seq * t_new, SSM_WIDTH), lambda i: (0, 0)),
            full3,
        ],
        out_shape=[
            jax.ShapeDtypeStruct((n_seq * t_new, SSM_WIDTH), F32),
            jax.ShapeDtypeStruct((n_seq, SSM_WIDTH, D_STATE), F32),
        ],
        compiler_params=_params("arbitrary"),
        name="ssd_sample",
    )(proj, proj, proj, proj, prev, h0, cw, cb, dtb, alog, valid, drow, g, tri, lsel, e)


def _out_proj_kernel(h_ref, ap_ref, sp_ref, as_ref, ss_ref, wa_ref, ws_ref, o_ref, *, n_prompt_tiles):
    i = pl.program_id(0)

    def mix(att, ssm):
        o_ref[...] = h_ref[...] + (_dot(att.astype(BF16), wa_ref[...]) + _dot(ssm.astype(BF16), ws_ref[...]))

    @pl.when(i < n_prompt_tiles)
    def _():
        mix(ap_ref[...], sp_ref[...])

    @pl.when(i >= n_prompt_tiles)
    def _():
        pad = jnp.zeros((h_ref.shape[0] - as_ref.shape[0], as_ref.shape[1]), F32)
        mix(jnp.concatenate([as_ref[...], pad], axis=0), jnp.concatenate([ss_ref[...], pad], axis=0))


def _out_proj(h, o_att_p, o_ssm_p, o_att_s, o_ssm_s, w, layer):
    t = h.shape[0]
    tm = ROW_ALIGN
    n_prompt, n_sample = o_att_p.shape[0], o_att_s.shape[0]
    n_prompt_tiles = n_prompt // tm
    assert n_prompt % tm == 0 and t // tm == n_prompt_tiles + 1 and n_sample <= tm
    prompt_spec = pl.BlockSpec((tm, ATT_WIDTH), lambda i: (jnp.minimum(i, n_prompt_tiles - 1), 0))
    sample_spec = pl.BlockSpec((n_sample, ATT_WIDTH), lambda i: (0, 0))
    return pl.pallas_call(
        functools.partial(_out_proj_kernel, n_prompt_tiles=n_prompt_tiles),
        grid=(t // tm,),
        in_specs=[
            pl.BlockSpec((tm, D_MODEL), lambda i: (i, 0)),
            prompt_spec, prompt_spec, sample_spec, sample_spec,
            pl.BlockSpec((None, ATT_WIDTH, D_MODEL), lambda i: (layer, 0, 0)),
            pl.BlockSpec((None, SSM_WIDTH, D_MODEL), lambda i: (layer, 1, 0)),
        ],
        out_specs=pl.BlockSpec((tm, D_MODEL), lambda i: (i, 0)),
        out_shape=jax.ShapeDtypeStruct((t, D_MODEL), F32),
        compiler_params=_params("arbitrary"),
        name="out_proj",
    )(h, o_att_p, o_ssm_p, o_att_s, o_ssm_s, w, w)


def _mlp_kernel(h_ref, g_ref, wu_ref, wd_ref, o_ref, hn_ref):
    @pl.when(pl.program_id(1) == 0)
    def _():
        h = h_ref[...]
        hn_ref[...] = _rms(h, g_ref[...]).astype(BF16)
        o_ref[...] = h

    u = jnp.maximum(_dot(hn_ref[...], wu_ref[...]), 0.0)
    o_ref[...] += _dot((u * u).astype(BF16), wd_ref[...])


def _mlp(h, g, wu, wd, layer):
    t = h.shape[0]
    tm = TOKEN_TILE
    tf = 512
    return pl.pallas_call(
        _mlp_kernel,
        grid=(t // tm, D_FF // tf),
        in_specs=[
            pl.BlockSpec((tm, D_MODEL), lambda i, f: (i, 0)),
            pl.BlockSpec((1, D_MODEL), lambda i, f: (0, 0)),
            pl.BlockSpec((None, D_MODEL, tf), lambda i, f: (layer, 0, f)),
            pl.BlockSpec((None, tf, D_MODEL), lambda i, f: (layer, f, 0)),
        ],
        out_specs=pl.BlockSpec((tm, D_MODEL), lambda i, f: (i, 0)),
        out_shape=jax.ShapeDtypeStruct((t, D_MODEL), F32),
        scratch_shapes=[pltpu.VMEM((tm, D_MODEL), BF16)],
        compiler_params=_params("parallel", "arbitrary"),
        name="mlp",
    )(h, g, wu, wd)


def _ple_update(h_ref, p_ref, g_ref, wg_ref, wp_ref):
    h = h_ref[...]
    gate = _sigmoid(_dot(_rms(h, g_ref[...]).astype(BF16), wg_ref[...]))
    return h + _dot(p_ref[...].astype(BF16), wp_ref[...]) * gate


def _ple_kernel(h_ref, p_ref, g_ref, wg_ref, wp_ref, o_ref):
    o_ref[...] = _ple_update(h_ref, p_ref, g_ref, wg_ref, wp_ref)


def _ple_final_kernel(h_ref, p_ref, g_ref, wg_ref, wp_ref, gf_ref, yp_ref, ys_ref, *, n_prompt_tiles):
    y = _rms(_ple_update(h_ref, p_ref, g_ref, wg_ref, wp_ref), gf_ref[...])
    i = pl.program_id(0)

    @pl.when(i < n_prompt_tiles)
    def _():
        yp_ref[...] = y

    @pl.when(i >= n_prompt_tiles)
    def _():
        ys_ref[...] = y[0:ys_ref.shape[0], :]


def _ple(h, p, g, wg, wp, layer, g_final=None, n_prompt=None, n_sample=None):
    t = h.shape[0]
    tm = ROW_ALIGN
    ple_dim = p.shape[-1]
    row = lambda: pl.BlockSpec((1, D_MODEL), lambda i: (0, 0))
    in_specs = [
        pl.BlockSpec((tm, D_MODEL), lambda i: (i, 0)),
        pl.BlockSpec((None, tm, ple_dim), lambda i: (layer, i, 0)),
        row(),
        pl.BlockSpec((None, D_MODEL, D_MODEL), lambda i: (layer, 0, 0)),
        pl.BlockSpec((None, ple_dim, D_MODEL), lambda i: (layer, 0, 0)),
    ]
    if g_final is None:
        return pl.pallas_call(
            _ple_kernel,
            grid=(t // tm,),
            in_specs=in_specs,
            out_specs=pl.BlockSpec((tm, D_MODEL), lambda i: (i, 0)),
            out_shape=jax.ShapeDtypeStruct((t, D_MODEL), F32),
            compiler_params=_params("parallel"),
            name="ple",
        )(h, p, g, wg, wp)
    n_prompt_tiles = n_prompt // tm
    assert n_prompt % tm == 0 and t // tm == n_prompt_tiles + 1 and n_sample <= tm
    return pl.pallas_call(
        functools.partial(_ple_final_kernel, n_prompt_tiles=n_prompt_tiles),
        grid=(t // tm,),
        in_specs=in_specs + [row()],
        out_specs=[
            pl.BlockSpec((tm, D_MODEL), lambda i: (jnp.minimum(i, n_prompt_tiles - 1), 0)),
            pl.BlockSpec((n_sample, D_MODEL), lambda i: (0, 0)),
        ],
        out_shape=[jax.ShapeDtypeStruct((n_prompt, D_MODEL), F32),
                   jax.ShapeDtypeStruct((n_sample, D_MODEL), F32)],
        compiler_params=_params("arbitrary"),
        name="ple_final",
    )(h, p, g, wg, wp, g_final)


def _stack_rows_kernel(*refs, n_layers, n_rows):
    srcs, dsts, sem = refs[:2 * n_layers], refs[2 * n_layers:2 * n_layers + 2], refs[-1]

    def layer_copy(i):
        return pltpu.make_async_copy(srcs[i].at[pl.ds(0, n_rows)], dsts[i // n_layers].at[i % n_layers],
                                     sem.at[i])

    for i in range(2 * n_layers):
        layer_copy(i).start()
    for i in range(2 * n_layers):
        layer_copy(i).wait()


def _stack_rows(k_layers, v_layers, n_rows):
    n_layers = len(k_layers)
    out = jax.ShapeDtypeStruct((n_layers, n_rows) + k_layers[0].shape[1:], k_layers[0].dtype)
    any_spec = pl.BlockSpec(memory_space=pl.ANY)
    return pl.pallas_call(
        functools.partial(_stack_rows_kernel, n_layers=n_layers, n_rows=n_rows),
        in_specs=[any_spec] * (2 * n_layers),
        out_specs=[any_spec, any_spec],
        out_shape=[out, out],
        scratch_shapes=[pltpu.SemaphoreType.DMA((2 * n_layers,))],
        name="stack_kv",
    )(*k_layers, *v_layers)


def _misc_row(values, lane0):
    row = jnp.zeros((1, LANES), F32)
    return row.at[0, lane0:lane0 + values.shape[0]].set(values.astype(F32))


def kernel(x_prompt, x_sample, cache_k, cache_v, cache_logf, state_ssm, state_conv, page_table, p_prompt, p_sample, norm_mix, w_in, b_forget, conv_w, conv_b, dt_bias, a_log, d_skip, norm_attn_out, norm_ssm_out, w_out, norm_mlp, w_up, w_down, norm_ple, w_ple_gate, w_ple_proj, norm_final):
    batch, seq, _ = x_prompt.shape
    n_seq, t_new, _ = x_sample.shape
    depth = w_in.shape[0]
    n_prompt = batch * seq
    n_sample = n_seq * t_new
    t_pad = -(-(n_prompt + CHUNK) // TOKEN_TILE) * TOKEN_TILE
    n_tail = t_pad - n_prompt - n_sample
    n_phys, page = cache_k.shape[1], cache_k.shape[2]

    h = jnp.concatenate([x_prompt.reshape(n_prompt, D_MODEL), x_sample.reshape(n_sample, D_MODEL),
                         jnp.zeros((n_tail, D_MODEL), F32)], axis=0)
    ple_dim = p_prompt.shape[-1]
    p_all = jnp.concatenate([p_prompt.reshape(depth, n_prompt, ple_dim),
                             p_sample.reshape(depth, n_sample, ple_dim),
                             jnp.zeros((depth, n_tail, ple_dim), F32)], axis=1)
    w_in_r = _reorder_w_in(w_in)
    w_out_b = w_out.astype(BF16)
    w_up_b = w_up.astype(BF16)
    w_down_b = w_down.astype(BF16)
    w_gate_b = w_ple_gate.astype(BF16)
    w_proj_b = w_ple_proj.astype(BF16)

    tri_full, _ = _ssd_consts(CHUNK)
    tri_seq, _ = _ssd_consts(t_new)
    tri2 = jnp.stack([tri_full, tri_seq])
    expand = _expand_matrix()
    valid_row = _misc_row(jnp.ones((N_SSM_HEADS,), F32), DT_LANE)
    pbias = _past_bias(cache_logf, page_table)
    causal_new = jnp.arange(t_new)[None, :] <= jnp.arange(t_new)[:, None]
    same_head = jnp.eye(N_ATT_HEADS, dtype=bool)

    outs = {name: [] for name in ("kp", "vp", "fp", "sp", "cp", "ks", "vs", "fs", "ss", "cs")}
    sample_rows = slice(n_prompt, n_prompt + n_sample)
    n_new = t_new * N_ATT_HEADS
    for l in range(depth):
        proj, k_rows, v_rows, k_new, v_new = _in_proj(h, norm_mix[l][None, :], w_in_r, l, n_prompt, n_sample)
        logf, cum, cumt = _gates(proj, _misc_row(b_forget[l], F_LANE), tri2, n_prompt, seq)

        o_att_p = _fox_prompt(proj, cum, cumt, norm_attn_out[l][None, :], batch, seq)
        qc = cumt[:, n_prompt:n_prompt + n_sample].reshape(N_ATT_HEADS, n_seq, t_new)
        qc = jnp.transpose(qc, (1, 0, 2))
        qc_col = qc.reshape(n_seq, N_ATT_HEADS * t_new, 1)
        bias_new = jnp.where(causal_new[None, None, :, :, None] & same_head[None, :, None, None, :],
                             -qc[:, :, None, :, None], NEG)
        bias_new = jnp.concatenate(
            [bias_new.reshape(n_seq, N_ATT_HEADS * t_new, t_new * N_ATT_HEADS),
             jnp.full((n_seq, N_ATT_HEADS * t_new, LANES - t_new * N_ATT_HEADS), NEG, F32)], axis=2)
        o_att_s = _fox_sample(proj, k_new.reshape(n_seq, n_new, ATT_HEAD_DIM),
                              v_new.reshape(n_seq, n_new, ATT_HEAD_DIM), qc_col, bias_new,
                              pbias, cache_k, cache_v, page_table, norm_attn_out[l][None, :], l,
                              n_prompt, t_new)

        dtb = _misc_row(dt_bias[l], DT_LANE)
        alog = _misc_row(a_log[l], DT_LANE)
        drow = jnp.repeat(d_skip[l].astype(F32), SSM_HEADDIM)[None, :]
        ssm_args = (conv_w[l], conv_b[l][None, :], dtb, alog, valid_row, drow,
                    norm_ssm_out[l][None, :], expand)
        o_ssm_p, st_p = _ssd_prompt(proj, *ssm_args, batch, seq)
        prev = jnp.concatenate(
            [jnp.zeros((n_seq, t_new - (CONV_W - 1), CONV_DIM), F32), state_conv[l]], axis=1)
        prev = jnp.concatenate([prev.reshape(n_sample, CONV_DIM),
                                jnp.zeros((CHUNK - n_sample, CONV_DIM), F32)], axis=0)
        h0 = state_ssm[l].reshape(n_seq, SSM_WIDTH, D_STATE)
        o_ssm_s, st_s = _ssd_sample(proj, prev, h0, *ssm_args, n_prompt, n_seq, t_new)

        h = _out_proj(h, o_att_p, o_ssm_p, o_att_s, o_ssm_s, w_out_b, l)
        h = _mlp(h, norm_mlp[l][None, :], w_up_b, w_down_b, l)
        ple_args = (h, p_all, norm_ple[l][None, :], w_gate_b, w_proj_b, l)
        if l < depth - 1:
            h = _ple(*ple_args)
        else:
            y_prompt, y_sample = _ple(*ple_args, norm_final[None, :], n_prompt, n_sample)

        xbc = slice(COL_XS, COL_XS + CONV_DIM)
        outs["kp"].append(k_rows)
        outs["vp"].append(v_rows)
        outs["ks"].append(k_new)
        outs["vs"].append(v_new)
        outs["fp"].append(logf[:n_prompt].reshape(batch, seq, N_ATT_HEADS))
        outs["sp"].append(st_p.reshape(batch, N_SSM_HEADS, SSM_HEADDIM, D_STATE))
        outs["cp"].append(jnp.stack(
            [proj[(b + 1) * seq - (CONV_W - 1):(b + 1) * seq, xbc] for b in range(batch)]))
        outs["fs"].append(logf[sample_rows].reshape(n_seq, t_new, N_ATT_HEADS))
        outs["ss"].append(st_s.reshape(n_seq, N_SSM_HEADS, SSM_HEADDIM, D_STATE))
        outs["cs"].append(jnp.stack(
            [proj[n_prompt + (b + 1) * t_new - (CONV_W - 1):n_prompt + (b + 1) * t_new, xbc]
             for b in range(n_seq)]))

    kv_shape_p = (depth, batch, seq, N_ATT_HEADS, ATT_HEAD_DIM)
    kv_shape_s = (depth, n_seq, t_new, N_ATT_HEADS, ATT_HEAD_DIM)
    stack = lambda name: jnp.stack(outs[name])
    k_prompt, v_prompt = _stack_rows(outs["kp"], outs["vp"], n_prompt)
    return (y_prompt.reshape(batch, seq, D_MODEL), y_sample.reshape(n_seq, t_new, D_MODEL),
            k_prompt.reshape(kv_shape_p), v_prompt.reshape(kv_shape_p),
            stack("fp"), stack("sp"), stack("cp"),
            stack("ks").reshape(kv_shape_s), stack("vs").reshape(kv_shape_s),
            stack("fs"), stack("ss"), stack("cs"))
```

```python
import functools

import numpy as np
import jax
import jax.numpy as jnp
from jax import lax
from jax.experimental import pallas as pl
from jax.experimental.pallas import tpu as pltpu

F32 = jnp.float32
BF16 = jnp.bfloat16

D_MODEL = 2048
ATT_HEAD_DIM = 128
ATT_WIDTH = D_MODEL // 2
N_ATT_HEADS = ATT_WIDTH // ATT_HEAD_DIM
ATT_SCALE = ATT_HEAD_DIM ** -0.5
SSM_WIDTH = D_MODEL - ATT_WIDTH
SSM_HEADDIM = 64
N_SSM_HEADS = SSM_WIDTH // SSM_HEADDIM
N_SSM_GROUPS = 2
GROUP_WIDTH = SSM_WIDTH // N_SSM_GROUPS
D_STATE = 128
CONV_W = 4
BC_WIDTH = 2 * N_SSM_GROUPS * D_STATE
CONV_DIM = SSM_WIDTH + BC_WIDTH
D_FF = 4 * D_MODEL
RMS_EPS = 1e-6
NEG = -1e30

SUBLANES = 8
LANES = 128
VMEM_LIMIT = 56 * 1024 * 1024

COL_Q = 0
COL_K = ATT_WIDTH
COL_V = 2 * ATT_WIDTH
COL_Z = 3 * ATT_WIDTH
COL_XS = COL_Z + SSM_WIDTH
COL_BC = COL_XS + SSM_WIDTH
COL_MISC = COL_BC + BC_WIDTH
PROJ_TILE = ATT_WIDTH
PROJ_W = 6 * PROJ_TILE
F_LANE = 0
DT_LANE = N_ATT_HEADS

CHUNK = 128
TOKEN_TILE = 768
ROW_ALIGN = 256


def _rms(x, g):
    ms = jnp.mean(x * x, axis=-1, keepdims=True)
    return (x * lax.rsqrt(ms + RMS_EPS)) * g


def _sigmoid(x):
    return 1.0 / (1.0 + jnp.exp(-x))


def _softplus(x):
    return jnp.maximum(x, 0.0) + jnp.log1p(jnp.exp(-jnp.abs(x)))


def _dot(a, b):
    return jnp.dot(a, b, preferred_element_type=F32)


def _bf16_terms(x):
    hi = x.astype(BF16)
    rest = x - hi.astype(F32)
    mid = rest.astype(BF16)
    lo = (rest - mid.astype(F32)).astype(BF16)
    return hi, mid, lo


def _select_rows(sel, x):
    return functools.reduce(lambda a, b: a + b, [_dot(sel, t) for t in _bf16_terms(x)])


def _select_cols(x, sel):
    return functools.reduce(lambda a, b: a + b, [_dot(t, sel) for t in _bf16_terms(x)])


def _dot_nt(a, b):
    return lax.dot_general(a, b, (((1,), (1,)), ((), ())), preferred_element_type=F32)


def _tile_lanes(x, rep):
    return x if rep == 1 else jnp.concatenate([x] * rep, axis=1)


def _params(*semantics):
    return pltpu.CompilerParams(dimension_semantics=semantics, vmem_limit_bytes=VMEM_LIMIT)


IN_F = 3 * ATT_WIDTH
IN_Z = IN_F + N_ATT_HEADS
IN_DT = IN_Z + SSM_WIDTH + CONV_DIM
IN_PROJ_DIM = IN_DT + N_SSM_HEADS


IN_LAST = IN_PROJ_DIM - PROJ_TILE


def _reorder_w_in_kernel(w_ref, f_ref, o_ref):
    j = pl.program_id(1)

    @pl.when(j < PROJ_W // PROJ_TILE - 1)
    def _():
        o_ref[...] = w_ref[0].T.astype(BF16)

    @pl.when(j == PROJ_W // PROJ_TILE - 1)
    def _():
        w = w_ref[0]
        bc0 = IN_DT - BC_WIDTH - IN_LAST
        pad = jnp.zeros((PROJ_W - COL_MISC - N_ATT_HEADS - N_SSM_HEADS, w.shape[1]), F32)
        rows = jnp.concatenate([w[bc0:bc0 + BC_WIDTH], f_ref[0], w[bc0 + BC_WIDTH:], pad], axis=0)
        o_ref[...] = rows.T.astype(BF16)


def _reorder_w_in(w_in):
    depth = w_in.shape[0]
    assert w_in.shape[2] == IN_PROJ_DIM
    n_tiles = PROJ_W // PROJ_TILE

    def first_row(j):
        shifted = j * PROJ_TILE + N_ATT_HEADS
        row = jnp.where(j < IN_F // PROJ_TILE, j * PROJ_TILE, jnp.where(j < n_tiles - 1, shifted, IN_LAST))
        return pl.multiple_of(row, SUBLANES)

    return pl.pallas_call(
        _reorder_w_in_kernel,
        grid=(depth, n_tiles),
        in_specs=[
            pl.BlockSpec((pl.Element(1), pl.Element(PROJ_TILE), pl.Element(D_MODEL)),
                         lambda l, j: (l, first_row(j), 0)),
            pl.BlockSpec((pl.Element(1), pl.Element(N_ATT_HEADS), pl.Element(D_MODEL)),
                         lambda l, j: (l, IN_F, 0)),
        ],
        out_specs=pl.BlockSpec((None, D_MODEL, PROJ_TILE), lambda l, j: (l, 0, j)),
        out_shape=jax.ShapeDtypeStruct((depth, D_MODEL, PROJ_W), BF16),
        compiler_params=_params("parallel", "parallel"),
        name="w_in_reorder",
    )(jnp.swapaxes(w_in, 1, 2), jnp.swapaxes(w_in, 1, 2))


def _in_proj_kernel(x_ref, g_ref, w_ref, *rest, n_prompt, n_sample):
    o_ref, kp_ref, vp_ref, ks_ref, vs_ref, xn_ref = rest
    i = pl.program_id(0)
    j = pl.program_id(1)
    tm = x_ref.shape[0]

    @pl.when(j == 0)
    def _():
        xn_ref[...] = _rms(x_ref[...], g_ref[...]).astype(BF16)

    res = _dot(xn_ref[...], w_ref[...])
    o_ref[...] = res

    def emit(prompt_ref, sample_ref):
        prompt_ref[...] = res.reshape(tm, N_ATT_HEADS, ATT_HEAD_DIM)

        @pl.when(i == n_prompt // tm)
        def _():
            r0 = n_prompt % tm
            sample_ref[...] = res[r0:r0 + n_sample, :].reshape(n_sample, N_ATT_HEADS, ATT_HEAD_DIM)

    @pl.when(j == COL_K // PROJ_TILE)
    def _():
        emit(kp_ref, ks_ref)

    @pl.when(j == COL_V // PROJ_TILE)
    def _():
        emit(vp_ref, vs_ref)


def _in_proj(h, g, w, layer, n_prompt, n_sample):
    t = h.shape[0]
    tm = TOKEN_TILE
    assert n_prompt % tm + n_sample <= tm
    head_shape = (N_ATT_HEADS, ATT_HEAD_DIM)
    cache_shapes = [(t,) + head_shape] * 2 + [(n_sample,) + head_shape] * 2
    cache_specs = [pl.BlockSpec((tm,) + head_shape, lambda i, j: (i, 0, 0))] * 2 + \
                  [pl.BlockSpec((n_sample,) + head_shape, lambda i, j: (0, 0, 0))] * 2
    return pl.pallas_call(
        functools.partial(_in_proj_kernel, n_prompt=n_prompt, n_sample=n_sample),
        grid=(t // tm, PROJ_W // PROJ_TILE),
        in_specs=[
            pl.BlockSpec((tm, D_MODEL), lambda i, j: (i, 0)),
            pl.BlockSpec((1, D_MODEL), lambda i, j: (0, 0)),
            pl.BlockSpec((None, D_MODEL, PROJ_TILE), lambda i, j: (layer, 0, j)),
        ],
        out_specs=[pl.BlockSpec((tm, PROJ_TILE), lambda i, j: (i, j))] + cache_specs,
        out_shape=[jax.ShapeDtypeStruct((t, PROJ_W), F32)]
                  + [jax.ShapeDtypeStruct(s, F32) for s in cache_shapes],
        scratch_shapes=[pltpu.VMEM((tm, D_MODEL), BF16)],
        compiler_params=_params("arbitrary", "arbitrary"),
        name="in_proj",
    )(h, g, w)


def _gates_kernel(misc_ref, fb_ref, tri_ref, logf_ref, cum_ref, cumt_ref, carry_ref, *,
                  chunks_per_seq, n_prompt_chunks):
    i = pl.program_id(0)

    @pl.when(jnp.logical_or(lax.rem(i, chunks_per_seq) == 0, i >= n_prompt_chunks))
    def _():
        carry_ref[...] = jnp.zeros_like(carry_ref)

    x = misc_ref[...] + fb_ref[...]
    lf = jnp.minimum(x, 0.0) - jnp.log1p(jnp.exp(-jnp.abs(x)))
    c = _select_rows(tri_ref[...], lf) + carry_ref[...]
    carry_ref[...] = c[CHUNK - 1:CHUNK, :]
    logf_ref[...] = lf[:, F_LANE:F_LANE + N_ATT_HEADS]
    cum_ref[...] = c[:, F_LANE:F_LANE + N_ATT_HEADS]
    cumt_ref[...] = c.T[F_LANE:F_LANE + N_ATT_HEADS, :]


def _gates(proj, fb_row, tri2, n_prompt_rows, seq):
    n_prompt_chunks = n_prompt_rows // CHUNK
    n_chunks = n_prompt_chunks + 1
    rows = n_chunks * CHUNK
    kern = functools.partial(_gates_kernel, chunks_per_seq=seq // CHUNK,
                             n_prompt_chunks=n_prompt_chunks)
    return pl.pallas_call(
        kern,
        grid=(n_chunks,),
        in_specs=[
            pl.BlockSpec((CHUNK, LANES), lambda i: (i, COL_MISC // LANES)),
            pl.BlockSpec((1, LANES), lambda i: (0, 0)),
            pl.BlockSpec((None, CHUNK, CHUNK), lambda i: (i // n_prompt_chunks, 0, 0)),
        ],
        out_specs=[
            pl.BlockSpec((CHUNK, N_ATT_HEADS), lambda i: (i, 0)),
            pl.BlockSpec((CHUNK, N_ATT_HEADS), lambda i: (i, 0)),
            pl.BlockSpec((N_ATT_HEADS, CHUNK), lambda i: (0, i)),
        ],
        out_shape=[
            jax.ShapeDtypeStruct((rows, N_ATT_HEADS), F32),
            jax.ShapeDtypeStruct((rows, N_ATT_HEADS), F32),
            jax.ShapeDtypeStruct((N_ATT_HEADS, rows), F32),
        ],
        scratch_shapes=[pltpu.VMEM((1, LANES), F32)],
        compiler_params=_params("arbitrary"),
        name="gates",
    )(proj, fb_row, tri2)


def _fox_prompt_kernel(qi_ref, ki_ref, q_ref, k_ref, v_ref, cq_ref, ck_ref, g_ref, o_ref,
                       m_ref, l_ref, acc_ref, cqb_ref, *, tq, tk):
    p = pl.program_id(1)
    qi = qi_ref[p]
    ki = ki_ref[p]
    rep = tk // LANES

    @pl.when(ki == 0)
    def _():
        m_ref[...] = jnp.full_like(m_ref, NEG)
        l_ref[...] = jnp.zeros_like(l_ref)
        acc_ref[...] = jnp.zeros_like(acc_ref)
        cq = cq_ref[...]
        for h in range(N_ATT_HEADS):
            cqb_ref[h] = jnp.broadcast_to(cq[:, h:h + 1], (tq, LANES))

    def step(on_diagonal):
        ck = ck_ref[...]
        if on_diagonal:
            visible = (lax.broadcasted_iota(jnp.int32, (tq, tk), 1)
                       <= lax.broadcasted_iota(jnp.int32, (tq, tk), 0))
        def scores(h):
            sl = slice(h * ATT_HEAD_DIM, (h + 1) * ATT_HEAD_DIM)
            return _dot_nt(q_ref[:, sl].astype(BF16), k_ref[:, sl].astype(BF16))

        s_next = scores(0)
        for h in range(N_ATT_HEADS):
            sl = slice(h * ATT_HEAD_DIM, (h + 1) * ATT_HEAD_DIM)
            s = s_next
            if h + 1 < N_ATT_HEADS:
                s_next = scores(h + 1)
            s = s * ATT_SCALE + (_tile_lanes(cqb_ref[h], rep) - ck[h:h + 1, :])
            if on_diagonal:
                s = jnp.where(visible, s, NEG)
            m_prev = m_ref[h]
            m_new = jnp.maximum(m_prev, jnp.max(s, axis=-1, keepdims=True))
            alpha = jnp.exp(m_prev - m_new)
            pm = jnp.exp(s - _tile_lanes(m_new, rep))
            l_ref[h] = alpha * l_ref[h] + jnp.sum(pm, axis=-1, keepdims=True)
            acc_ref[:, sl] = alpha * acc_ref[:, sl] + _dot(pm.astype(BF16), v_ref[:, sl].astype(BF16))
            m_ref[h] = m_new

    @pl.when(ki < qi)
    def _():
        step(False)

    @pl.when(ki == qi)
    def _():
        step(True)
        for h in range(N_ATT_HEADS):
            sl = slice(h * ATT_HEAD_DIM, (h + 1) * ATT_HEAD_DIM)
            acc_ref[:, sl] = acc_ref[:, sl] / l_ref[h]
        o_ref[...] = _rms(acc_ref[...], g_ref[...])


def _fox_prompt(proj, cum, cumt, g, batch, seq):
    tq = tk = min(512, seq)
    nq = seq // tq
    pairs = [(a, b) for a in range(nq) for b in range(a + 1)]
    qi_tab = jnp.asarray(np.array([a for a, _ in pairs], np.int32))
    ki_tab = jnp.asarray(np.array([b for _, b in pairs], np.int32))
    kern = functools.partial(_fox_prompt_kernel, tq=tq, tk=tk)
    grid_spec = pltpu.PrefetchScalarGridSpec(
        num_scalar_prefetch=2,
        grid=(batch, len(pairs)),
        in_specs=[
            pl.BlockSpec((tq, ATT_WIDTH), lambda b, p, qt, kt: (b * nq + qt[p], COL_Q // ATT_WIDTH)),
            pl.BlockSpec((tk, ATT_WIDTH), lambda b, p, qt, kt: (b * nq + kt[p], COL_K // ATT_WIDTH)),
            pl.BlockSpec((tk, ATT_WIDTH), lambda b, p, qt, kt: (b * nq + kt[p], COL_V // ATT_WIDTH)),
            pl.BlockSpec((tq, N_ATT_HEADS), lambda b, p, qt, kt: (b * nq + qt[p], 0)),
            pl.BlockSpec((N_ATT_HEADS, tk), lambda b, p, qt, kt: (0, b * nq + kt[p])),
            pl.BlockSpec((1, ATT_WIDTH), lambda b, p, qt, kt: (0, 0)),
        ],
        out_specs=pl.BlockSpec((tq, ATT_WIDTH), lambda b, p, qt, kt: (b * nq + qt[p], 0)),
        scratch_shapes=[
            pltpu.VMEM((N_ATT_HEADS, tq, LANES), F32),
            pltpu.VMEM((N_ATT_HEADS, tq, LANES), F32),
            pltpu.VMEM((tq, ATT_WIDTH), F32),
            pltpu.VMEM((N_ATT_HEADS, tq, LANES), F32),
        ],
    )
    return pl.pallas_call(
        kern,
        grid_spec=grid_spec,
        out_shape=jax.ShapeDtypeStruct((batch * seq, ATT_WIDTH), F32),
        compiler_params=_params("parallel", "arbitrary"),
        name="fox_prompt",
    )(qi_tab, ki_tab, proj, proj, proj, cum, cumt, g)


def _past_bias_kernel(pt_ref, logf_hbm, m1_ref, s_ref, su_ref, e8_ref, o_ref, x_ref, sem, *,
                      n_pages):
    l = pl.program_id(0)
    b = pl.program_id(1)

    def page_copy(p):
        page = pt_ref[b * n_pages + p]
        return pltpu.make_async_copy(logf_hbm.at[l, pl.ds(page, 1), :], x_ref.at[pl.ds(p, 1), :], sem)

    def start(p, carry):
        page_copy(p).start()
        return carry

    def wait(p, carry):
        page_copy(p).wait()
        return carry

    lax.fori_loop(0, n_pages, start, 0)
    lax.fori_loop(0, n_pages, wait, 0)
    x = x_ref[...]
    within = _select_cols(x, m1_ref[...])
    totals = _select_cols(x, s_ref[...])
    later = _select_rows(su_ref[...], totals)
    o_ref[...] = within + _select_cols(later, e8_ref[...])


def _past_bias(cache_logf, page_table):
    depth, n_phys, page, heads = cache_logf.shape
    n_seq, n_pages = page_table.shape
    width = page * heads
    r = lax.broadcasted_iota(jnp.int32, (width, width), 0)
    c = lax.broadcasted_iota(jnp.int32, (width, width), 1)
    m1 = ((r % heads == c % heads) & (r // heads > c // heads)).astype(BF16)
    r = lax.broadcasted_iota(jnp.int32, (width, LANES), 0)
    c = lax.broadcasted_iota(jnp.int32, (width, LANES), 1)
    s = (r % heads == c).astype(BF16)
    r = lax.broadcasted_iota(jnp.int32, (n_pages, n_pages), 0)
    c = lax.broadcasted_iota(jnp.int32, (n_pages, n_pages), 1)
    su = (c > r).astype(BF16)
    r = lax.broadcasted_iota(jnp.int32, (LANES, width), 0)
    c = lax.broadcasted_iota(jnp.int32, (LANES, width), 1)
    e8 = (r == c % heads).astype(BF16)
    const = lambda shape: pl.BlockSpec(shape, lambda l, b, pt: (0, 0))
    grid_spec = pltpu.PrefetchScalarGridSpec(
        num_scalar_prefetch=1,
        grid=(depth, n_seq),
        in_specs=[
            pl.BlockSpec(memory_space=pl.ANY),
            const((width, width)), const((width, LANES)), const((n_pages, n_pages)),
            const((LANES, width)),
        ],
        out_specs=pl.BlockSpec((None, None, n_pages, width), lambda l, b, pt: (l, b, 0, 0)),
        scratch_shapes=[pltpu.VMEM((n_pages, width), F32), pltpu.SemaphoreType.DMA(())],
    )
    out = pl.pallas_call(
        functools.partial(_past_bias_kernel, n_pages=n_pages),
        grid_spec=grid_spec,
        out_shape=jax.ShapeDtypeStruct((depth, n_seq, n_pages, width), F32),
        compiler_params=_params("arbitrary", "arbitrary"),
        name="past_bias",
    )(page_table.reshape(-1), cache_logf.reshape(depth, n_phys, width), m1, s, su, e8)
    return out


def _fox_sample_kernel(pt_ref, q_ref, kn_ref, vn_ref, qc_ref, bn_ref, pb_ref, g_ref, *rest,
                       pages_per_step, t_new):
    k_refs = rest[:pages_per_step]
    v_refs = rest[pages_per_step:2 * pages_per_step]
    o_ref, qall_ref, qcm_ref, kpad_ref, vpad_ref, m_ref, l_ref, acc_ref = rest[2 * pages_per_step:]
    p = pl.program_id(1)
    rows, width = qcm_ref.shape

    @pl.when(p == 0)
    def _():
        m_ref[...] = jnp.full_like(m_ref, NEG)
        l_ref[...] = jnp.zeros_like(l_ref)
        acc_ref[...] = jnp.zeros_like(acc_ref)
        for h in range(N_ATT_HEADS):
            qall_ref[h * t_new:(h + 1) * t_new, :] = q_ref[:, h * ATT_HEAD_DIM:(h + 1) * ATT_HEAD_DIM]
        head_of_row = lax.broadcasted_iota(jnp.int32, (rows, width), 0) // t_new
        head_of_lane = lax.rem(lax.broadcasted_iota(jnp.int32, (rows, width), 1), N_ATT_HEADS)
        qcm_ref[...] = qc_ref[...] + jnp.where(head_of_row == head_of_lane, 0.0, NEG)

    qall = qall_ref[...].astype(BF16)

    def attend(pages):
        scores = [_dot_nt(qall, k_flat.astype(BF16)) * ATT_SCALE + bias for k_flat, _, bias in pages]
        m_cur = functools.reduce(jnp.maximum, [jnp.max(s, axis=-1, keepdims=True) for s in scores])
        m_prev = m_ref[...]
        m_new = jnp.maximum(m_prev, m_cur)
        alpha = jnp.exp(m_prev - m_new)
        l_new = alpha * l_ref[...]
        acc_new = alpha * acc_ref[...]
        for s, (_, v_flat, _) in zip(scores, pages):
            pm = jnp.exp(s - _tile_lanes(m_new, s.shape[1] // LANES))
            l_new = l_new + jnp.sum(pm, axis=-1, keepdims=True)
            acc_new = acc_new + _dot(pm.astype(BF16), v_flat.astype(BF16))
        l_ref[...] = l_new
        acc_ref[...] = acc_new
        m_ref[...] = m_new

    qcm = qcm_ref[...]
    attend([(k_refs[i][...].reshape(width, ATT_HEAD_DIM), v_refs[i][...].reshape(width, ATT_HEAD_DIM),
             qcm + pb_ref[i:i + 1, :]) for i in range(pages_per_step)])

    @pl.when(p == pl.num_programs(1) - 1)
    def _():
        n_new = kn_ref.shape[0]
        kpad_ref[...] = jnp.zeros_like(kpad_ref)
        vpad_ref[...] = jnp.zeros_like(vpad_ref)
        kpad_ref[0:n_new, :] = kn_ref[...]
        vpad_ref[0:n_new, :] = vn_ref[...]
        attend([(kpad_ref[...], vpad_ref[...], qc_ref[...] + bn_ref[...])])
        o_full = acc_ref[...] / l_ref[...]
        o = jnp.concatenate([o_full[h * t_new:(h + 1) * t_new, :] for h in range(N_ATT_HEADS)], axis=1)
        o_ref[...] = _rms(o, g_ref[...])


def _fox_sample(proj, k_new, v_new, qc_col, bias_new, pbias, cache_k, cache_v, page_table, g, layer,
                row0, t_new):
    n_seq, n_pages = page_table.shape
    page = cache_k.shape[2]
    width = page * N_ATT_HEADS
    pages_per_step = 16
    steps = n_pages // pages_per_step
    rows = N_ATT_HEADS * t_new
    n_new = t_new * N_ATT_HEADS
    rb0 = row0 // t_new

    def page_spec(i):
        return pl.BlockSpec(
            (None, None, page, N_ATT_HEADS, ATT_HEAD_DIM),
            lambda b, p, pt: (layer, pt[b * n_pages + p * pages_per_step + i], 0, 0, 0))

    per_seq = lambda shape: pl.BlockSpec((None,) + shape, lambda b, p, pt: (b, 0, 0))
    new_rows = lambda: per_seq((n_new, ATT_HEAD_DIM))
    grid_spec = pltpu.PrefetchScalarGridSpec(
        num_scalar_prefetch=1,
        grid=(n_seq, steps),
        in_specs=[
            pl.BlockSpec((t_new, ATT_WIDTH), lambda b, p, pt: (rb0 + b, COL_Q // ATT_WIDTH)),
            new_rows(), new_rows(),
            per_seq((rows, 1)), per_seq((rows, LANES)),
            pl.BlockSpec((None, None, pages_per_step, width), lambda b, p, pt: (layer, b, p, 0)),
            pl.BlockSpec((1, ATT_WIDTH), lambda b, p, pt: (0, 0)),
        ] + [page_spec(i) for i in range(pages_per_step)] * 2,
        out_specs=pl.BlockSpec((t_new, ATT_WIDTH), lambda b, p, pt: (b, 0)),
        scratch_shapes=[
            pltpu.VMEM((rows, ATT_HEAD_DIM), F32),
            pltpu.VMEM((rows, width), F32),
            pltpu.VMEM((LANES, ATT_HEAD_DIM), F32),
            pltpu.VMEM((LANES, ATT_HEAD_DIM), F32),
            pltpu.VMEM((rows, LANES), F32),
            pltpu.VMEM((rows, LANES), F32),
            pltpu.VMEM((rows, ATT_HEAD_DIM), F32),
        ],
    )
    kern = functools.partial(_fox_sample_kernel, pages_per_step=pages_per_step, t_new=t_new)
    return pl.pallas_call(
        kern,
        grid_spec=grid_spec,
        out_shape=jax.ShapeDtypeStruct((n_seq * t_new, ATT_WIDTH), F32),
        compiler_params=_params("parallel", "arbitrary"),
        name="fox_sample",
    )(page_table.reshape(-1), proj, k_new, v_new, qc_col, bias_new, pbias, g,
      *([cache_k] * pages_per_step), *([cache_v] * pages_per_step))


def _causal_conv(x, prev, w, b, rows_per_seq):
    n = x.shape[0]
    row = lax.broadcasted_iota(jnp.int32, x.shape, 0)
    out = None
    for j in range(CONV_W - 1, 0, -1):
        if rows_per_seq is None:
            shifted = jnp.where(row >= j, pltpu.roll(x, j, 0), pltpu.roll(prev, j, 0))
        else:
            shifted = jnp.where(lax.rem(row, rows_per_seq) >= j, pltpu.roll(x, j, 0),
                                pltpu.roll(prev, (j - rows_per_seq) % n, 0))
        tap = CONV_W - 1 - j
        term = shifted * w[tap:tap + 1, :]
        out = term if out is None else out + term
    out = out + x * w[CONV_W - 1:CONV_W, :]
    out = out + b
    return out * _sigmoid(out)


def _ssd_scalars(misc, dtb, alog, valid, tri, lsel):
    dtf = _softplus(misc + dtb)
    a = dtf * (-(jnp.exp(alog) * valid))
    acum = _select_rows(tri, a)
    if lsel is None:
        acum_end = acum[CHUNK - 1:CHUNK, :]
    else:
        acum_end = _select_rows(lsel, acum)
    return dtf, acum, acum_end


def _ssd_diag(xs, bm, cm, dtf, acum, visible):
    acum_t = acum.T
    dt_t = dtf.T
    lane = lax.broadcasted_iota(jnp.int32, (xs.shape[0], LANES), 1)
    heads_per_group = N_SSM_HEADS // N_SSM_GROUPS
    blocks = []
    for g in range(N_SSM_GROUPS):
        gs = slice(g * D_STATE, (g + 1) * D_STATE)
        cb = _dot_nt(cm[:, gs].astype(BF16), bm[:, gs].astype(BF16))
        for j in range(heads_per_group // 2):
            blk = g * (heads_per_group // 2) + j
            xpair = xs[:, blk * LANES:(blk + 1) * LANES]
            y = None
            for hh in range(2):
                ch = DT_LANE + 2 * blk + hh
                seg = acum[:, ch:ch + 1] - acum_t[ch:ch + 1, :]
                decay = jnp.exp(jnp.where(visible, seg, NEG))
                m = (cb * decay) * dt_t[ch:ch + 1, :]
                in_half = (lane >= SSM_HEADDIM) if hh else (lane < SSM_HEADDIM)
                xh = jnp.where(in_half, xpair, 0.0)
                part = _dot(m.astype(BF16), xh.astype(BF16))
                y = part if y is None else y + part
            blocks.append(y)
    return jnp.concatenate(blocks, axis=1)


def _ssd_prompt_kernel(z_ref, xs_ref, bc_ref, misc_ref, cw_ref, cb_ref, dtb_ref, alog_ref,
                       valid_ref, drow_ref, g_ref, tri_ref, e_ref,
                       o_ref, st_ref, ht_ref, pxs_ref, pbc_ref, *, chunks_per_seq):
    c = lax.rem(pl.program_id(0), chunks_per_seq)

    @pl.when(c == 0)
    def _():
        ht_ref[...] = jnp.zeros_like(ht_ref)
        pxs_ref[...] = jnp.zeros_like(pxs_ref)
        pbc_ref[...] = jnp.zeros_like(pbc_ref)

    xs_raw = xs_ref[...]
    bc_raw = bc_ref[...]
    cw = cw_ref[...]
    cb = cb_ref[...]
    xs = _causal_conv(xs_raw, pxs_ref[...], cw[:, :SSM_WIDTH], cb[:, :SSM_WIDTH], None)
    bc = _causal_conv(bc_raw, pbc_ref[...], cw[:, SSM_WIDTH:], cb[:, SSM_WIDTH:], None)
    pxs_ref[...] = xs_raw
    pbc_ref[...] = bc_raw
    bm = bc[:, :N_SSM_GROUPS * D_STATE]
    cm = bc[:, N_SSM_GROUPS * D_STATE:]

    dtf, acum, acum_end = _ssd_scalars(misc_ref[...], dtb_ref[...], alog_ref[...], valid_ref[...],
                                       tri_ref[...], None)
    row = lax.broadcasted_iota(jnp.int32, (CHUNK, CHUNK), 0)
    col = lax.broadcasted_iota(jnp.int32, (CHUNK, CHUNK), 1)
    y = _ssd_diag(xs, bm, cm, dtf, acum, col <= row)

    e = e_ref[...]
    exp_a = jnp.exp(_select_cols(acum, e))
    w_end = _select_cols(jnp.exp(acum_end - acum) * dtf, e)
    xw = xs * w_end
    ht = ht_ref[...]
    y_off = []
    new_states = []
    for g in range(N_SSM_GROUPS):
        gs = slice(g * D_STATE, (g + 1) * D_STATE)
        ws = slice(g * GROUP_WIDTH, (g + 1) * GROUP_WIDTH)
        y_off.append(_dot(cm[:, gs].astype(BF16), ht[:, ws].astype(BF16)))
        new_states.append(_dot(bm[:, gs].T.astype(BF16), xw[:, ws].astype(BF16)))
    y = y + jnp.concatenate(y_off, axis=1) * exp_a
    ht_new = exp_a[CHUNK - 1:CHUNK, :] * ht + jnp.concatenate(new_states, axis=1)
    ht_ref[...] = ht_new

    y = y + drow_ref[...] * xs
    z = z_ref[...]
    y = y * (z * _sigmoid(z))
    o_ref[...] = _rms(y, g_ref[...])

    @pl.when(c == chunks_per_seq - 1)
    def _():
        for j in range(SSM_WIDTH // LANES):
            st_ref[j * LANES:(j + 1) * LANES, :] = ht_new[:, j * LANES:(j + 1) * LANES].T


def _ssd_consts(l_rows_per_seq):
    r = lax.broadcasted_iota(jnp.int32, (CHUNK, CHUNK), 0)
    c = lax.broadcasted_iota(jnp.int32, (CHUNK, CHUNK), 1)
    same = (r // l_rows_per_seq) == (c // l_rows_per_seq)
    tri = (same & (c <= r)).astype(BF16)
    lsel = (c == (r // l_rows_per_seq) * l_rows_per_seq + l_rows_per_seq - 1).astype(BF16)
    return tri, lsel


def _expand_matrix():
    r = lax.broadcasted_iota(jnp.int32, (LANES, SSM_WIDTH), 0)
    c = lax.broadcasted_iota(jnp.int32, (LANES, SSM_WIDTH), 1)
    return (r == DT_LANE + c // SSM_HEADDIM).astype(BF16)


def _row_spec(shape):
    return pl.BlockSpec(shape, lambda i: (0, 0))


def _ssd_prompt(proj, cw, cb, dtb, alog, valid, drow, g, e, batch, seq):
    chunks_per_seq = seq // CHUNK
    tri, _ = _ssd_consts(CHUNK)
    kern = functools.partial(_ssd_prompt_kernel, chunks_per_seq=chunks_per_seq)
    return pl.pallas_call(
        kern,
        grid=(batch * chunks_per_seq,),
        in_specs=[
            pl.BlockSpec((CHUNK, SSM_WIDTH), lambda i: (i, COL_Z // SSM_WIDTH)),
            pl.BlockSpec((CHUNK, SSM_WIDTH), lambda i: (i, COL_XS // SSM_WIDTH)),
            pl.BlockSpec((CHUNK, BC_WIDTH), lambda i: (i, COL_BC // BC_WIDTH)),
            pl.BlockSpec((CHUNK, LANES), lambda i: (i, COL_MISC // LANES)),
            _row_spec((CONV_W, CONV_DIM)), _row_spec((1, CONV_DIM)),
            _row_spec((1, LANES)), _row_spec((1, LANES)), _row_spec((1, LANES)),
            _row_spec((1, SSM_WIDTH)), _row_spec((1, SSM_WIDTH)),
            _row_spec((CHUNK, CHUNK)), _row_spec((LANES, SSM_WIDTH)),
        ],
        out_specs=[
            pl.BlockSpec((CHUNK, SSM_WIDTH), lambda i: (i, 0)),
            pl.BlockSpec((None, SSM_WIDTH, D_STATE), lambda i: (i // chunks_per_seq, 0, 0)),
        ],
        out_shape=[
            jax.ShapeDtypeStruct((batch * seq, SSM_WIDTH), F32),
            jax.ShapeDtypeStruct((batch, SSM_WIDTH, D_STATE), F32),
        ],
        scratch_shapes=[
            pltpu.VMEM((D_STATE, SSM_WIDTH), F32),
            pltpu.VMEM((CHUNK, SSM_WIDTH), F32),
            pltpu.VMEM((CHUNK, BC_WIDTH), F32),
        ],
        compiler_params=_params("arbitrary"),
        name="ssd_prompt",
    )(proj, proj, proj, proj, cw, cb, dtb, alog, valid, drow, g, tri, e)


def _ssd_sample_kernel(z_ref, xs_ref, bc_ref, misc_ref, prev_ref, h0_ref, cw_ref, cb_ref, dtb_ref,
                       alog_ref, valid_ref, drow_ref, g_ref, tri_ref, lsel_ref, e_ref,
                       o_ref, st_ref, *, n_seq, t_new):
    n_rows = n_seq * t_new
    live = lax.broadcasted_iota(jnp.int32, (CHUNK, 1), 0) < n_rows
    xs_raw = jnp.where(live, xs_ref[...], 0.0)
    bc_raw = jnp.where(live, bc_ref[...], 0.0)
    z = jnp.where(live, z_ref[...], 0.0)
    misc = jnp.where(live, misc_ref[...], 0.0)
    cw = cw_ref[...]
    cb = cb_ref[...]
    prev = prev_ref[...]
    xs = _causal_conv(xs_raw, prev[:, :SSM_WIDTH], cw[:, :SSM_WIDTH], cb[:, :SSM_WIDTH], t_new)
    bc = _causal_conv(bc_raw, prev[:, SSM_WIDTH:], cw[:, SSM_WIDTH:], cb[:, SSM_WIDTH:], t_new)
    bm = bc[:, :N_SSM_GROUPS * D_STATE]
    cm = bc[:, N_SSM_GROUPS * D_STATE:]

    dtf, acum, acum_end = _ssd_scalars(misc, dtb_ref[...], alog_ref[...], valid_ref[...],
                                       tri_ref[...], lsel_ref[...])
    row = lax.broadcasted_iota(jnp.int32, (CHUNK, CHUNK), 0)
    col = lax.broadcasted_iota(jnp.int32, (CHUNK, CHUNK), 1)
    visible = (row // t_new == col // t_new) & (col <= row)
    y = _ssd_diag(xs, bm, cm, dtf, acum, visible)

    e = e_ref[...]
    exp_a = jnp.exp(_select_cols(acum, e))
    w_end = _select_cols(jnp.exp(acum_end - acum) * dtf, e)
    xw = xs * w_end
    bm_t = [bm[:, g * D_STATE:(g + 1) * D_STATE].T for g in range(N_SSM_GROUPS)]
    lane_seq = lax.broadcasted_iota(jnp.int32, (D_STATE, CHUNK), 1) // t_new

    y_off_rows = []
    for b in range(n_seq):
        rows = slice(b * t_new, (b + 1) * t_new)
        h0 = h0_ref[b]
        ht = jnp.concatenate(
            [h0[j * LANES:(j + 1) * LANES, :].T for j in range(SSM_WIDTH // LANES)], axis=1)
        y_off = []
        new_states = []
        for g in range(N_SSM_GROUPS):
            gs = slice(g * D_STATE, (g + 1) * D_STATE)
            ws = slice(g * GROUP_WIDTH, (g + 1) * GROUP_WIDTH)
            y_off.append(_dot(cm[:, gs].astype(BF16), ht[:, ws].astype(BF16))[rows, :])
            bt = jnp.where(lane_seq == b, bm_t[g], 0.0)
            new_states.append(_dot(bt.astype(BF16), xw[:, ws].astype(BF16)))
        y_off_rows.append(jnp.concatenate(y_off, axis=1))
        last = (b + 1) * t_new - 1
        ht_new = exp_a[last:last + 1, :] * ht + jnp.concatenate(new_states, axis=1)
        for j in range(SSM_WIDTH // LANES):
            st_ref[b, j * LANES:(j + 1) * LANES, :] = ht_new[:, j * LANES:(j + 1) * LANES].T
    y_off_rows.append(jnp.zeros((CHUNK - n_rows, SSM_WIDTH), F32))
    y = y + jnp.concatenate(y_off_rows, axis=0) * exp_a

    y = y + drow_ref[...] * xs
    y = y * (z * _sigmoid(z))
    o_ref[...] = _rms(y, g_ref[...])[0:n_rows, :]


def _ssd_sample(proj, prev, h0, cw, cb, dtb, alog, valid, drow, g, e, row0, n_seq, t_new):
    tri, lsel = _ssd_consts(t_new)
    rb = row0 // CHUNK
    kern = functools.partial(_ssd_sample_kernel, n_seq=n_seq, t_new=t_new)
    full3 = pl.BlockSpec((n_seq, SSM_WIDTH, D_STATE), lambda i: (0, 0, 0))
    return pl.pallas_call(
        kern,
        grid=(1,),
        in_specs=[
            pl.BlockSpec((CHUNK, SSM_WIDTH), lambda i: (rb, COL_Z // SSM_WIDTH)),
            pl.BlockSpec((CHUNK, SSM_WIDTH), lambda i: (rb, COL_XS // SSM_WIDTH)),
            pl.BlockSpec((CHUNK, BC_WIDTH), lambda i: (rb, COL_BC // BC_WIDTH)),
            pl.BlockSpec((CHUNK, LANES), lambda i: (rb, COL_MISC // LANES)),
            _row_spec((CHUNK, CONV_DIM)),
            full3,
            _row_spec((CONV_W, CONV_DIM)), _row_spec((1, CONV_DIM)),
            _row_spec((1, LANES)), _row_spec((1, LANES)), _row_spec((1, LANES)),
            _row_spec((1, SSM_WIDTH)), _row_spec((1, SSM_WIDTH)),
            _row_spec((CHUNK, CHUNK)), _row_spec((CHUNK, CHUNK)), _row_spec((LANES, SSM_WIDTH)),
        ],
        out_specs=[
            pl.BlockSpec((n_seq * t_new, SSM_WIDTH), lambda i: (0, 0)),
            full3,
        ],
        out_shape=[
            jax.ShapeDtypeStruct((n_seq * t_new, SSM_WIDTH), F32),
            jax.ShapeDtypeStruct((n_seq, SSM_WIDTH, D_STATE), F32),
        ],
        compiler_params=_params("arbitrary"),
        name="ssd_sample",
    )(proj, proj, proj, proj, prev, h0, cw, cb, dtb, alog, valid, drow, g, tri, lsel, e)


def _out_proj_kernel(h_ref, ap_ref, sp_ref, as_ref, ss_ref, wa_ref, ws_ref, o_ref, *, n_prompt_tiles):
    i = pl.program_id(0)

    def mix(att, ssm):
        o_ref[...] = h_ref[...] + (_dot(att.astype(BF16), wa_ref[...]) + _dot(ssm.astype(BF16), ws_ref[...]))

    @pl.when(i < n_prompt_tiles)
    def _():
        mix(ap_ref[...], sp_ref[...])

    @pl.when(i >= n_prompt_tiles)
    def _():
        pad = jnp.zeros((h_ref.shape[0] - as_ref.shape[0], as_ref.shape[1]), F32)
        mix(jnp.concatenate([as_ref[...], pad], axis=0), jnp.concatenate([ss_ref[...], pad], axis=0))


def _out_proj(h, o_att_p, o_ssm_p, o_att_s, o_ssm_s, w, layer):
    t = h.shape[0]
    tm = ROW_ALIGN
    n_prompt, n_sample = o_att_p.shape[0], o_att_s.shape[0]
    n_prompt_tiles = n_prompt // tm
    assert n_prompt % tm == 0 and t // tm == n_prompt_tiles + 1 and n_sample <= tm
    prompt_spec = pl.BlockSpec((tm, ATT_WIDTH), lambda i: (jnp.minimum(i, n_prompt_tiles - 1), 0))
    sample_spec = pl.BlockSpec((n_sample, ATT_WIDTH), lambda i: (0, 0))
    return pl.pallas_call(
        functools.partial(_out_proj_kernel, n_prompt_tiles=n_prompt_tiles),
        grid=(t // tm,),
        in_specs=[
            pl.BlockSpec((tm, D_MODEL), lambda i: (i, 0)),
            prompt_spec, prompt_spec, sample_spec, sample_spec,
            pl.BlockSpec((None, ATT_WIDTH, D_MODEL), lambda i: (layer, 0, 0)),
            pl.BlockSpec((None, SSM_WIDTH, D_MODEL), lambda i: (layer, 1, 0)),
        ],
        out_specs=pl.BlockSpec((tm, D_MODEL), lambda i: (i, 0)),
        out_shape=jax.ShapeDtypeStruct((t, D_MODEL), F32),
        compiler_params=_params("arbitrary"),
        name="out_proj",
    )(h, o_att_p, o_ssm_p, o_att_s, o_ssm_s, w, w)


def _mlp_kernel(h_ref, g_ref, wu_ref, wd_ref, o_ref, hn_ref):
    @pl.when(pl.program_id(1) == 0)
    def _():
        h = h_ref[...]
        hn_ref[...] = _rms(h, g_ref[...]).astype(BF16)
        o_ref[...] = h

    u = jnp.maximum(_dot(hn_ref[...], wu_ref[...]), 0.0)
    o_ref[...] += _dot((u * u).astype(BF16), wd_ref[...])


def _mlp(h, g, wu, wd, layer):
    t = h.shape[0]
    tm = TOKEN_TILE
    tf = 1024
    return pl.pallas_call(
        _mlp_kernel,
        grid=(t // tm, D_FF // tf),
        in_specs=[
            pl.BlockSpec((tm, D_MODEL), lambda i, f: (i, 0)),
            pl.BlockSpec((1, D_MODEL), lambda i, f: (0, 0)),
            pl.BlockSpec((None, D_MODEL, tf), lambda i, f: (layer, 0, f)),
            pl.BlockSpec((None, tf, D_MODEL), lambda i, f: (layer, f, 0)),
        ],
        out_specs=pl.BlockSpec((tm, D_MODEL), lambda i, f: (i, 0)),
        out_shape=jax.ShapeDtypeStruct((t, D_MODEL), F32),
        scratch_shapes=[pltpu.VMEM((tm, D_MODEL), BF16)],
        compiler_params=_params("parallel", "arbitrary"),
        name="mlp",
    )(h, g, wu, wd)


def _ple_update(h_ref, p_ref, g_ref, wg_ref, wp_ref):
    h = h_ref[...]
    gate = _sigmoid(_dot(_rms(h, g_ref[...]).astype(BF16), wg_ref[...]))
    return h + _dot(p_ref[...].astype(BF16), wp_ref[...]) * gate


def _ple_kernel(h_ref, p_ref, g_ref, wg_ref, wp_ref, o_ref):
    o_ref[...] = _ple_update(h_ref, p_ref, g_ref, wg_ref, wp_ref)


def _ple_final_kernel(h_ref, p_ref, g_ref, wg_ref, wp_ref, gf_ref, yp_ref, ys_ref, *, n_prompt_tiles):
    y = _rms(_ple_update(h_ref, p_ref, g_ref, wg_ref, wp_ref), gf_ref[...])
    i = pl.program_id(0)

    @pl.when(i < n_prompt_tiles)
    def _():
        yp_ref[...] = y

    @pl.when(i >= n_prompt_tiles)
    def _():
        ys_ref[...] = y[0:ys_ref.shape[0], :]


def _ple(h, p, g, wg, wp, layer, g_final=None, n_prompt=None, n_sample=None):
    t = h.shape[0]
    tm = ROW_ALIGN
    ple_dim = p.shape[-1]
    row = lambda: pl.BlockSpec((1, D_MODEL), lambda i: (0, 0))
    in_specs = [
        pl.BlockSpec((tm, D_MODEL), lambda i: (i, 0)),
        pl.BlockSpec((None, tm, ple_dim), lambda i: (layer, i, 0)),
        row(),
        pl.BlockSpec((None, D_MODEL, D_MODEL), lambda i: (layer, 0, 0)),
        pl.BlockSpec((None, ple_dim, D_MODEL), lambda i: (layer, 0, 0)),
    ]
    if g_final is None:
        return pl.pallas_call(
            _ple_kernel,
            grid=(t // tm,),
            in_specs=in_specs,
            out_specs=pl.BlockSpec((tm, D_MODEL), lambda i: (i, 0)),
            out_shape=jax.ShapeDtypeStruct((t, D_MODEL), F32),
            compiler_params=_params("parallel"),
            name="ple",
        )(h, p, g, wg, wp)
    n_prompt_tiles = n_prompt // tm
    assert n_prompt % tm == 0 and t // tm == n_prompt_tiles + 1 and n_sample <= tm
    return pl.pallas_call(
        functools.partial(_ple_final_kernel, n_prompt_tiles=n_prompt_tiles),
        grid=(t // tm,),
        in_specs=in_specs + [row()],
        out_specs=[
            pl.BlockSpec((tm, D_MODEL), lambda i: (jnp.minimum(i, n_prompt_tiles - 1), 0)),
            pl.BlockSpec((n_sample, D_MODEL), lambda i: (0, 0)),
        ],
        out_shape=[jax.ShapeDtypeStruct((n_prompt, D_MODEL), F32),
                   jax.ShapeDtypeStruct((n_sample, D_MODEL), F32)],
        compiler_params=_params("arbitrary"),
        name="ple_final",
    )(h, p, g, wg, wp, g_final)


def _stack_rows_kernel(*refs, n_layers):
    k_refs, v_refs = refs[:n_layers], refs[n_layers:2 * n_layers]
    ko_ref, vo_ref = refs[2 * n_layers:]
    layer = pl.program_id(0)
    for n in range(n_layers):
        @pl.when(layer == n)
        def _():
            ko_ref[...] = k_refs[n][...]
            vo_ref[...] = v_refs[n][...]


def _stack_rows(k_layers, v_layers, n_rows):
    n_layers = len(k_layers)
    rows = 2 * ROW_ALIGN
    assert n_rows % rows == 0
    n_tiles = n_rows // rows
    tail = k_layers[0].shape[1:]
    zeros = (0,) * len(tail)
    out = jax.ShapeDtypeStruct((n_layers, n_rows) + tail, k_layers[0].dtype)

    def src_spec(n):
        return pl.BlockSpec((rows,) + tail,
                            lambda l, i: (jnp.where(l == n, i, jnp.where(l > n, n_tiles - 1, 0)),) + zeros)

    out_spec = pl.BlockSpec((None, rows) + tail, lambda l, i: (l, i) + zeros)
    return pl.pallas_call(
        functools.partial(_stack_rows_kernel, n_layers=n_layers),
        grid=(n_layers, n_tiles),
        in_specs=[src_spec(n) for n in range(n_layers)] * 2,
        out_specs=[out_spec, out_spec],
        out_shape=[out, out],
        compiler_params=_params("arbitrary", "arbitrary"),
        name="stack_kv",
    )(*k_layers, *v_layers)


def _misc_row(values, lane0):
    row = jnp.zeros((1, LANES), F32)
    return row.at[0, lane0:lane0 + values.shape[0]].set(values.astype(F32))


def kernel(x_prompt, x_sample, cache_k, cache_v, cache_logf, state_ssm, state_conv, page_table, p_prompt, p_sample, norm_mix, w_in, b_forget, conv_w, conv_b, dt_bias, a_log, d_skip, norm_attn_out, norm_ssm_out, w_out, norm_mlp, w_up, w_down, norm_ple, w_ple_gate, w_ple_proj, norm_final):
    batch, seq, _ = x_prompt.shape
    n_seq, t_new, _ = x_sample.shape
    depth = w_in.shape[0]
    n_prompt = batch * seq
    n_sample = n_seq * t_new
    t_pad = -(-(n_prompt + CHUNK) // TOKEN_TILE) * TOKEN_TILE
    n_tail = t_pad - n_prompt - n_sample
    n_phys, page = cache_k.shape[1], cache_k.shape[2]

    h = jnp.concatenate([x_prompt.reshape(n_prompt, D_MODEL), x_sample.reshape(n_sample, D_MODEL),
                         jnp.zeros((n_tail, D_MODEL), F32)], axis=0)
    ple_dim = p_prompt.shape[-1]
    p_all = jnp.concatenate([p_prompt.reshape(depth, n_prompt, ple_dim),
                             p_sample.reshape(depth, n_sample, ple_dim),
                             jnp.zeros((depth, n_tail, ple_dim), F32)], axis=1)
    w_in_r = _reorder_w_in(w_in)
    w_out_b = w_out.astype(BF16)
    w_up_b = w_up.astype(BF16)
    w_down_b = w_down.astype(BF16)
    w_gate_b = w_ple_gate.astype(BF16)
    w_proj_b = w_ple_proj.astype(BF16)

    tri_full, _ = _ssd_consts(CHUNK)
    tri_seq, _ = _ssd_consts(t_new)
    tri2 = jnp.stack([tri_full, tri_seq])
    expand = _expand_matrix()
    valid_row = _misc_row(jnp.ones((N_SSM_HEADS,), F32), DT_LANE)
    pbias = _past_bias(cache_logf, page_table)
    causal_new = jnp.arange(t_new)[None, :] <= jnp.arange(t_new)[:, None]
    same_head = jnp.eye(N_ATT_HEADS, dtype=bool)

    outs = {name: [] for name in ("kp", "vp", "fp", "sp", "cp", "ks", "vs", "fs", "ss", "cs")}
    sample_rows = slice(n_prompt, n_prompt + n_sample)
    n_new = t_new * N_ATT_HEADS
    for l in range(depth):
        proj, k_rows, v_rows, k_new, v_new = _in_proj(h, norm_mix[l][None, :], w_in_r, l, n_prompt, n_sample)
        logf, cum, cumt = _gates(proj, _misc_row(b_forget[l], F_LANE), tri2, n_prompt, seq)

        o_att_p = _fox_prompt(proj, cum, cumt, norm_attn_out[l][None, :], batch, seq)
        qc = cumt[:, n_prompt:n_prompt + n_sample].reshape(N_ATT_HEADS, n_seq, t_new)
        qc = jnp.transpose(qc, (1, 0, 2))
        qc_col = qc.reshape(n_seq, N_ATT_HEADS * t_new, 1)
        bias_new = jnp.where(causal_new[None, None, :, :, None] & same_head[None, :, None, None, :],
                             -qc[:, :, None, :, None], NEG)
        bias_new = jnp.concatenate(
            [bias_new.reshape(n_seq, N_ATT_HEADS * t_new, t_new * N_ATT_HEADS),
             jnp.full((n_seq, N_ATT_HEADS * t_new, LANES - t_new * N_ATT_HEADS), NEG, F32)], axis=2)
        o_att_s = _fox_sample(proj, k_new.reshape(n_seq, n_new, ATT_HEAD_DIM),
                              v_new.reshape(n_seq, n_new, ATT_HEAD_DIM), qc_col, bias_new,
                              pbias, cache_k, cache_v, page_table, norm_attn_out[l][None, :], l,
                              n_prompt, t_new)

        dtb = _misc_row(dt_bias[l], DT_LANE)
        alog = _misc_row(a_log[l], DT_LANE)
        drow = jnp.repeat(d_skip[l].astype(F32), SSM_HEADDIM)[None, :]
        ssm_args = (conv_w[l], conv_b[l][None, :], dtb, alog, valid_row, drow,
                    norm_ssm_out[l][None, :], expand)
        o_ssm_p, st_p = _ssd_prompt(proj, *ssm_args, batch, seq)
        prev = jnp.concatenate(
            [jnp.zeros((n_seq, t_new - (CONV_W - 1), CONV_DIM), F32), state_conv[l]], axis=1)
        prev = jnp.concatenate([prev.reshape(n_sample, CONV_DIM),
                                jnp.zeros((CHUNK - n_sample, CONV_DIM), F32)], axis=0)
        h0 = state_ssm[l].reshape(n_seq, SSM_WIDTH, D_STATE)
        o_ssm_s, st_s = _ssd_sample(proj, prev, h0, *ssm_args, n_prompt, n_seq, t_new)

        h = _out_proj(h, o_att_p, o_ssm_p, o_att_s, o_ssm_s, w_out_b, l)
        h = _mlp(h, norm_mlp[l][None, :], w_up_b, w_down_b, l)
        ple_args = (h, p_all, norm_ple[l][None, :], w_gate_b, w_proj_b, l)
        if l < depth - 1:
            h = _ple(*ple_args)
        else:
            y_prompt, y_sample = _ple(*ple_args, norm_final[None, :], n_prompt, n_sample)

        xbc = slice(COL_XS, COL_XS + CONV_DIM)
        outs["kp"].append(k_rows)
        outs["vp"].append(v_rows)
        outs["ks"].append(k_new)
        outs["vs"].append(v_new)
        outs["fp"].append(logf[:n_prompt].reshape(batch, seq, N_ATT_HEADS))
        outs["sp"].append(st_p.reshape(batch, N_SSM_HEADS, SSM_HEADDIM, D_STATE))
        outs["cp"].append(jnp.stack(
            [proj[(b + 1) * seq - (CONV_W - 1):(b + 1) * seq, xbc] for b in range(batch)]))
        outs["fs"].append(logf[sample_rows].reshape(n_seq, t_new, N_ATT_HEADS))
        outs["ss"].append(st_s.reshape(n_seq, N_SSM_HEADS, SSM_HEADDIM, D_STATE))
        outs["cs"].append(jnp.stack(
            [proj[n_prompt + (b + 1) * t_new - (CONV_W - 1):n_prompt + (b + 1) * t_new, xbc]
             for b in range(n_seq)]))

    kv_shape_p = (depth, batch, seq, N_ATT_HEADS, ATT_HEAD_DIM)
    kv_shape_s = (depth, n_seq, t_new, N_ATT_HEADS, ATT_HEAD_DIM)
    stack = lambda name: jnp.stack(outs[name])
    k_prompt, v_prompt = _stack_rows(outs["kp"], outs["vp"], n_prompt)
    return (y_prompt.reshape(batch, seq, D_MODEL), y_sample.reshape(n_seq, t_new, D_MODEL),
            k_prompt.reshape(kv_shape_p), v_prompt.reshape(kv_shape_p),
            stack("fp"), stack("sp"), stack("cp"),
            stack("ks").reshape(kv_shape_s), stack("vs").reshape(kv_shape_s),
            stack("fs"), stack("ss"), stack("cs"))
```

```python
import functools

import numpy as np
import jax
import jax.numpy as jnp
from jax import lax
from jax.experimental import pallas as pl
from jax.experimental.pallas import tpu as pltpu

F32 = jnp.float32
BF16 = jnp.bfloat16

D_MODEL = 2048
ATT_HEAD_DIM = 128
ATT_WIDTH = D_MODEL // 2
N_ATT_HEADS = ATT_WIDTH // ATT_HEAD_DIM
ATT_SCALE = ATT_HEAD_DIM ** -0.5
SSM_WIDTH = D_MODEL - ATT_WIDTH
SSM_HEADDIM = 64
N_SSM_HEADS = SSM_WIDTH // SSM_HEADDIM
N_SSM_GROUPS = 2
GROUP_WIDTH = SSM_WIDTH // N_SSM_GROUPS
D_STATE = 128
CONV_W = 4
BC_WIDTH = 2 * N_SSM_GROUPS * D_STATE
CONV_DIM = SSM_WIDTH + BC_WIDTH
D_FF = 4 * D_MODEL
RMS_EPS = 1e-6
NEG = -1e30

SUBLANES = 8
LANES = 128
VMEM_LIMIT = 56 * 1024 * 1024

COL_Q = 0
COL_K = ATT_WIDTH
COL_V = 2 * ATT_WIDTH
COL_Z = 3 * ATT_WIDTH
COL_XS = COL_Z + SSM_WIDTH
COL_BC = COL_XS + SSM_WIDTH
COL_MISC = COL_BC + BC_WIDTH
PROJ_TILE = ATT_WIDTH
PROJ_W = 6 * PROJ_TILE
F_LANE = 0
DT_LANE = N_ATT_HEADS

CHUNK = 128
TOKEN_TILE = 768
ROW_ALIGN = 256


def _rms(x, g):
    ms = jnp.mean(x * x, axis=-1, keepdims=True)
    return (x * lax.rsqrt(ms + RMS_EPS)) * g


def _sigmoid(x):
    return 1.0 / (1.0 + jnp.exp(-x))


def _softplus(x):
    return jnp.maximum(x, 0.0) + jnp.log1p(jnp.exp(-jnp.abs(x)))


def _dot(a, b):
    return jnp.dot(a, b, preferred_element_type=F32)


def _bf16_terms(x):
    hi = x.astype(BF16)
    rest = x - hi.astype(F32)
    mid = rest.astype(BF16)
    lo = (rest - mid.astype(F32)).astype(BF16)
    return hi, mid, lo


def _select_rows(sel, x):
    return functools.reduce(lambda a, b: a + b, [_dot(sel, t) for t in _bf16_terms(x)])


def _select_cols(x, sel):
    return functools.reduce(lambda a, b: a + b, [_dot(t, sel) for t in _bf16_terms(x)])


def _dot_nt(a, b):
    return lax.dot_general(a, b, (((1,), (1,)), ((), ())), preferred_element_type=F32)


def _tile_lanes(x, rep):
    return x if rep == 1 else jnp.concatenate([x] * rep, axis=1)


def _layer_spec(shape, layer):
    return pl.BlockSpec((None,) + shape, lambda *_: (layer,) + (0,) * len(shape))


def _params(*semantics):
    return pltpu.CompilerParams(dimension_semantics=semantics, vmem_limit_bytes=VMEM_LIMIT)


IN_F = 3 * ATT_WIDTH
IN_Z = IN_F + N_ATT_HEADS
IN_DT = IN_Z + SSM_WIDTH + CONV_DIM
IN_PROJ_DIM = IN_DT + N_SSM_HEADS


IN_LAST = IN_PROJ_DIM - PROJ_TILE


def _reorder_w_in_kernel(w_ref, f_ref, o_ref):
    j = pl.program_id(1)

    @pl.when(j < PROJ_W // PROJ_TILE - 1)
    def _():
        o_ref[...] = w_ref[0].T.astype(BF16)

    @pl.when(j == PROJ_W // PROJ_TILE - 1)
    def _():
        w = w_ref[0]
        bc0 = IN_DT - BC_WIDTH - IN_LAST
        pad = jnp.zeros((PROJ_W - COL_MISC - N_ATT_HEADS - N_SSM_HEADS, w.shape[1]), F32)
        rows = jnp.concatenate([w[bc0:bc0 + BC_WIDTH], f_ref[0], w[bc0 + BC_WIDTH:], pad], axis=0)
        o_ref[...] = rows.T.astype(BF16)


def _reorder_w_in(w_in):
    depth = w_in.shape[0]
    assert w_in.shape[2] == IN_PROJ_DIM
    n_tiles = PROJ_W // PROJ_TILE

    def first_row(j):
        shifted = j * PROJ_TILE + N_ATT_HEADS
        row = jnp.where(j < IN_F // PROJ_TILE, j * PROJ_TILE, jnp.where(j < n_tiles - 1, shifted, IN_LAST))
        return pl.multiple_of(row, SUBLANES)

    return pl.pallas_call(
        _reorder_w_in_kernel,
        grid=(depth, n_tiles),
        in_specs=[
            pl.BlockSpec((pl.Element(1), pl.Element(PROJ_TILE), pl.Element(D_MODEL)),
                         lambda l, j: (l, first_row(j), 0)),
            pl.BlockSpec((pl.Element(1), pl.Element(N_ATT_HEADS), pl.Element(D_MODEL)),
                         lambda l, j: (l, IN_F, 0)),
        ],
        out_specs=pl.BlockSpec((None, D_MODEL, PROJ_TILE), lambda l, j: (l, 0, j)),
        out_shape=jax.ShapeDtypeStruct((depth, D_MODEL, PROJ_W), BF16),
        compiler_params=_params("parallel", "parallel"),
        name="w_in_reorder",
    )(jnp.swapaxes(w_in, 1, 2), jnp.swapaxes(w_in, 1, 2))


def _in_proj_kernel(x_ref, g_ref, w_ref, *rest, n_prompt, n_sample):
    o_ref, kp_ref, vp_ref, ks_ref, vs_ref, xn_ref = rest
    i = pl.program_id(0)
    j = pl.program_id(1)
    tm = x_ref.shape[0]

    @pl.when(j == 0)
    def _():
        xn_ref[...] = _rms(x_ref[...], g_ref[...]).astype(BF16)

    res = _dot(xn_ref[...], w_ref[...])
    o_ref[...] = res

    def emit(prompt_ref, sample_ref):
        prompt_ref[...] = res.reshape(tm, N_ATT_HEADS, ATT_HEAD_DIM)

        @pl.when(i == n_prompt // tm)
        def _():
            r0 = n_prompt % tm
            sample_ref[...] = res[r0:r0 + n_sample, :].reshape(n_sample, N_ATT_HEADS, ATT_HEAD_DIM)

    @pl.when(j == COL_K // PROJ_TILE)
    def _():
        emit(kp_ref, ks_ref)

    @pl.when(j == COL_V // PROJ_TILE)
    def _():
        emit(vp_ref, vs_ref)


def _in_proj(h, g, w, layer, n_prompt, n_sample):
    t = h.shape[0]
    tm = TOKEN_TILE
    assert n_prompt % tm + n_sample <= tm
    head_shape = (N_ATT_HEADS, ATT_HEAD_DIM)
    cache_shapes = [(t,) + head_shape] * 2 + [(n_sample,) + head_shape] * 2
    cache_specs = [pl.BlockSpec((tm,) + head_shape, lambda i, j: (i, 0, 0))] * 2 + \
                  [pl.BlockSpec((n_sample,) + head_shape, lambda i, j: (0, 0, 0))] * 2
    return pl.pallas_call(
        functools.partial(_in_proj_kernel, n_prompt=n_prompt, n_sample=n_sample),
        grid=(t // tm, PROJ_W // PROJ_TILE),
        in_specs=[
            pl.BlockSpec((tm, D_MODEL), lambda i, j: (i, 0)),
            _layer_spec((1, D_MODEL), layer),
            pl.BlockSpec((None, D_MODEL, PROJ_TILE), lambda i, j: (layer, 0, j)),
        ],
        out_specs=[pl.BlockSpec((tm, PROJ_TILE), lambda i, j: (i, j))] + cache_specs,
        out_shape=[jax.ShapeDtypeStruct((t, PROJ_W), F32)]
                  + [jax.ShapeDtypeStruct(s, F32) for s in cache_shapes],
        scratch_shapes=[pltpu.VMEM((tm, D_MODEL), BF16)],
        compiler_params=_params("arbitrary", "arbitrary"),
        name="in_proj",
    )(h, g, w)


def _forget_gates(misc, fb, tri, carry):
    x = misc + fb
    lf = jnp.minimum(x, 0.0) - jnp.log1p(jnp.exp(-jnp.abs(x)))
    return lf, _select_rows(tri, lf) + carry


def _head_lanes(x):
    return x[:, F_LANE:F_LANE + N_ATT_HEADS]


def _fox_prompt_kernel(qi_ref, ki_ref, q_ref, k_ref, v_ref, cq_ref, ck_ref, g_ref, o_ref,
                       m_ref, l_ref, acc_ref, cqb_ref, *, tq, tk):
    p = pl.program_id(1)
    qi = qi_ref[p]
    ki = ki_ref[p]
    rep = tk // LANES

    @pl.when(ki == 0)
    def _():
        m_ref[...] = jnp.full_like(m_ref, NEG)
        l_ref[...] = jnp.zeros_like(l_ref)
        acc_ref[...] = jnp.zeros_like(acc_ref)
        cq = cq_ref[...]
        for h in range(N_ATT_HEADS):
            cqb_ref[h] = jnp.broadcast_to(cq[:, h:h + 1], (tq, LANES))

    def step(on_diagonal):
        ck = ck_ref[...]
        if on_diagonal:
            visible = (lax.broadcasted_iota(jnp.int32, (tq, tk), 1)
                       <= lax.broadcasted_iota(jnp.int32, (tq, tk), 0))
        def scores(h):
            sl = slice(h * ATT_HEAD_DIM, (h + 1) * ATT_HEAD_DIM)
            return _dot_nt(q_ref[:, sl].astype(BF16), k_ref[:, sl].astype(BF16))

        s_next = scores(0)
        for h in range(N_ATT_HEADS):
            sl = slice(h * ATT_HEAD_DIM, (h + 1) * ATT_HEAD_DIM)
            s = s_next
            if h + 1 < N_ATT_HEADS:
                s_next = scores(h + 1)
            s = s * ATT_SCALE + (_tile_lanes(cqb_ref[h], rep) - ck[h:h + 1, :])
            if on_diagonal:
                s = jnp.where(visible, s, NEG)
            m_prev = m_ref[h]
            m_new = jnp.maximum(m_prev, jnp.max(s, axis=-1, keepdims=True))
            alpha = jnp.exp(m_prev - m_new)
            pm = jnp.exp(s - _tile_lanes(m_new, rep))
            l_ref[h] = alpha * l_ref[h] + jnp.sum(pm, axis=-1, keepdims=True)
            acc_ref[:, sl] = alpha * acc_ref[:, sl] + _dot(pm.astype(BF16), v_ref[:, sl].astype(BF16))
            m_ref[h] = m_new

    @pl.when(ki < qi)
    def _():
        step(False)

    @pl.when(ki == qi)
    def _():
        step(True)
        for h in range(N_ATT_HEADS):
            sl = slice(h * ATT_HEAD_DIM, (h + 1) * ATT_HEAD_DIM)
            acc_ref[:, sl] = acc_ref[:, sl] / l_ref[h]
        o_ref[...] = _rms(acc_ref[...], g_ref[...])


def _fox_prompt(proj, cum, cumt, g, layer, batch, seq):
    tq = tk = min(512, seq)
    nq = seq // tq
    pairs = [(a, b) for a in range(nq) for b in range(a + 1)]
    qi_tab = jnp.asarray(np.array([a for a, _ in pairs], np.int32))
    ki_tab = jnp.asarray(np.array([b for _, b in pairs], np.int32))
    kern = functools.partial(_fox_prompt_kernel, tq=tq, tk=tk)
    grid_spec = pltpu.PrefetchScalarGridSpec(
        num_scalar_prefetch=2,
        grid=(batch, len(pairs)),
        in_specs=[
            pl.BlockSpec((tq, ATT_WIDTH), lambda b, p, qt, kt: (b * nq + qt[p], COL_Q // ATT_WIDTH)),
            pl.BlockSpec((tk, ATT_WIDTH), lambda b, p, qt, kt: (b * nq + kt[p], COL_K // ATT_WIDTH)),
            pl.BlockSpec((tk, ATT_WIDTH), lambda b, p, qt, kt: (b * nq + kt[p], COL_V // ATT_WIDTH)),
            pl.BlockSpec((tq, N_ATT_HEADS), lambda b, p, qt, kt: (b * nq + qt[p], 0)),
            pl.BlockSpec((N_ATT_HEADS, tk), lambda b, p, qt, kt: (0, b * nq + kt[p])),
            _layer_spec((1, ATT_WIDTH), layer),
        ],
        out_specs=pl.BlockSpec((tq, ATT_WIDTH), lambda b, p, qt, kt: (b * nq + qt[p], 0)),
        scratch_shapes=[
            pltpu.VMEM((N_ATT_HEADS, tq, LANES), F32),
            pltpu.VMEM((N_ATT_HEADS, tq, LANES), F32),
            pltpu.VMEM((tq, ATT_WIDTH), F32),
            pltpu.VMEM((N_ATT_HEADS, tq, LANES), F32),
        ],
    )
    return pl.pallas_call(
        kern,
        grid_spec=grid_spec,
        out_shape=jax.ShapeDtypeStruct((batch * seq, ATT_WIDTH), F32),
        compiler_params=_params("parallel", "arbitrary"),
        name="fox_prompt",
    )(qi_tab, ki_tab, proj, proj, proj, cum, cumt, g)


def _past_bias_kernel(pt_ref, logf_hbm, m1_ref, s_ref, su_ref, e8_ref, o_ref, x_ref, sem, *,
                      n_pages):
    l = pl.program_id(0)
    b = pl.program_id(1)

    def page_copy(p):
        page = pt_ref[b * n_pages + p]
        return pltpu.make_async_copy(logf_hbm.at[l, pl.ds(page, 1), :], x_ref.at[pl.ds(p, 1), :], sem)

    def start(p, carry):
        page_copy(p).start()
        return carry

    def wait(p, carry):
        page_copy(p).wait()
        return carry

    lax.fori_loop(0, n_pages, start, 0)
    lax.fori_loop(0, n_pages, wait, 0)
    x = x_ref[...]
    within = _select_cols(x, m1_ref[...])
    totals = _select_cols(x, s_ref[...])
    later = _select_rows(su_ref[...], totals)
    o_ref[...] = within + _select_cols(later, e8_ref[...])


def _past_bias(cache_logf, page_table):
    depth, n_phys, page, heads = cache_logf.shape
    n_seq, n_pages = page_table.shape
    width = page * heads
    r = lax.broadcasted_iota(jnp.int32, (width, width), 0)
    c = lax.broadcasted_iota(jnp.int32, (width, width), 1)
    m1 = ((r % heads == c % heads) & (r // heads > c // heads)).astype(BF16)
    r = lax.broadcasted_iota(jnp.int32, (width, LANES), 0)
    c = lax.broadcasted_iota(jnp.int32, (width, LANES), 1)
    s = (r % heads == c).astype(BF16)
    r = lax.broadcasted_iota(jnp.int32, (n_pages, n_pages), 0)
    c = lax.broadcasted_iota(jnp.int32, (n_pages, n_pages), 1)
    su = (c > r).astype(BF16)
    r = lax.broadcasted_iota(jnp.int32, (LANES, width), 0)
    c = lax.broadcasted_iota(jnp.int32, (LANES, width), 1)
    e8 = (r == c % heads).astype(BF16)
    const = lambda shape: pl.BlockSpec(shape, lambda l, b, pt: (0, 0))
    grid_spec = pltpu.PrefetchScalarGridSpec(
        num_scalar_prefetch=1,
        grid=(depth, n_seq),
        in_specs=[
            pl.BlockSpec(memory_space=pl.ANY),
            const((width, width)), const((width, LANES)), const((n_pages, n_pages)),
            const((LANES, width)),
        ],
        out_specs=pl.BlockSpec((None, None, n_pages, width), lambda l, b, pt: (l, b, 0, 0)),
        scratch_shapes=[pltpu.VMEM((n_pages, width), F32), pltpu.SemaphoreType.DMA(())],
    )
    out = pl.pallas_call(
        functools.partial(_past_bias_kernel, n_pages=n_pages),
        grid_spec=grid_spec,
        out_shape=jax.ShapeDtypeStruct((depth, n_seq, n_pages, width), F32),
        compiler_params=_params("arbitrary", "arbitrary"),
        name="past_bias",
    )(page_table.reshape(-1), cache_logf.reshape(depth, n_phys, width), m1, s, su, e8)
    return out


def _fox_sample_kernel(pt_ref, q_ref, kn_ref, vn_ref, qc_ref, bn_ref, pb_ref, g_ref, *rest,
                       pages_per_step, t_new):
    k_refs = rest[:pages_per_step]
    v_refs = rest[pages_per_step:2 * pages_per_step]
    o_ref, qall_ref, qcm_ref, kpad_ref, vpad_ref, m_ref, l_ref, acc_ref = rest[2 * pages_per_step:]
    p = pl.program_id(1)
    rows, width = qcm_ref.shape

    @pl.when(p == 0)
    def _():
        m_ref[...] = jnp.full_like(m_ref, NEG)
        l_ref[...] = jnp.zeros_like(l_ref)
        acc_ref[...] = jnp.zeros_like(acc_ref)
        for h in range(N_ATT_HEADS):
            qall_ref[h * t_new:(h + 1) * t_new, :] = q_ref[:, h * ATT_HEAD_DIM:(h + 1) * ATT_HEAD_DIM]
        head_of_row = lax.broadcasted_iota(jnp.int32, (rows, width), 0) // t_new
        head_of_lane = lax.rem(lax.broadcasted_iota(jnp.int32, (rows, width), 1), N_ATT_HEADS)
        qcm_ref[...] = qc_ref[...] + jnp.where(head_of_row == head_of_lane, 0.0, NEG)

    qall = qall_ref[...].astype(BF16)

    def attend(pages):
        scores = [_dot_nt(qall, k_flat.astype(BF16)) * ATT_SCALE + bias for k_flat, _, bias in pages]
        m_cur = functools.reduce(jnp.maximum, [jnp.max(s, axis=-1, keepdims=True) for s in scores])
        m_prev = m_ref[...]
        m_new = jnp.maximum(m_prev, m_cur)
        alpha = jnp.exp(m_prev - m_new)
        l_new = alpha * l_ref[...]
        acc_new = alpha * acc_ref[...]
        for s, (_, v_flat, _) in zip(scores, pages):
            pm = jnp.exp(s - _tile_lanes(m_new, s.shape[1] // LANES))
            l_new = l_new + jnp.sum(pm, axis=-1, keepdims=True)
            acc_new = acc_new + _dot(pm.astype(BF16), v_flat.astype(BF16))
        l_ref[...] = l_new
        acc_ref[...] = acc_new
        m_ref[...] = m_new

    qcm = qcm_ref[...]
    attend([(k_refs[i][...].reshape(width, ATT_HEAD_DIM), v_refs[i][...].reshape(width, ATT_HEAD_DIM),
             qcm + pb_ref[i:i + 1, :]) for i in range(pages_per_step)])

    @pl.when(p == pl.num_programs(1) - 1)
    def _():
        n_new = kn_ref.shape[0]
        kpad_ref[...] = jnp.zeros_like(kpad_ref)
        vpad_ref[...] = jnp.zeros_like(vpad_ref)
        kpad_ref[0:n_new, :] = kn_ref[...]
        vpad_ref[0:n_new, :] = vn_ref[...]
        attend([(kpad_ref[...], vpad_ref[...], qc_ref[...] + bn_ref[...])])
        o_full = acc_ref[...] / l_ref[...]
        o = jnp.concatenate([o_full[h * t_new:(h + 1) * t_new, :] for h in range(N_ATT_HEADS)], axis=1)
        o_ref[...] = _rms(o, g_ref[...])


def _fox_sample(proj, k_new, v_new, qc_col, bias_new, pbias, cache_k, cache_v, page_table, g, layer,
                row0, t_new):
    n_seq, n_pages = page_table.shape
    page = cache_k.shape[2]
    width = page * N_ATT_HEADS
    pages_per_step = 16
    steps = n_pages // pages_per_step
    rows = N_ATT_HEADS * t_new
    n_new = t_new * N_ATT_HEADS
    rb0 = row0 // t_new

    def page_spec(i):
        return pl.BlockSpec(
            (None, None, page, N_ATT_HEADS, ATT_HEAD_DIM),
            lambda b, p, pt: (layer, pt[b * n_pages + p * pages_per_step + i], 0, 0, 0))

    per_seq = lambda shape: pl.BlockSpec((None,) + shape, lambda b, p, pt: (b, 0, 0))
    new_rows = lambda: per_seq((n_new, ATT_HEAD_DIM))
    grid_spec = pltpu.PrefetchScalarGridSpec(
        num_scalar_prefetch=1,
        grid=(n_seq, steps),
        in_specs=[
            pl.BlockSpec((t_new, ATT_WIDTH), lambda b, p, pt: (rb0 + b, COL_Q // ATT_WIDTH)),
            new_rows(), new_rows(),
            per_seq((rows, 1)), per_seq((rows, LANES)),
            pl.BlockSpec((None, None, pages_per_step, width), lambda b, p, pt: (layer, b, p, 0)),
            _layer_spec((1, ATT_WIDTH), layer),
        ] + [page_spec(i) for i in range(pages_per_step)] * 2,
        out_specs=pl.BlockSpec((t_new, ATT_WIDTH), lambda b, p, pt: (b, 0)),
        scratch_shapes=[
            pltpu.VMEM((rows, ATT_HEAD_DIM), F32),
            pltpu.VMEM((rows, width), F32),
            pltpu.VMEM((LANES, ATT_HEAD_DIM), F32),
            pltpu.VMEM((LANES, ATT_HEAD_DIM), F32),
            pltpu.VMEM((rows, LANES), F32),
            pltpu.VMEM((rows, LANES), F32),
            pltpu.VMEM((rows, ATT_HEAD_DIM), F32),
        ],
    )
    kern = functools.partial(_fox_sample_kernel, pages_per_step=pages_per_step, t_new=t_new)
    return pl.pallas_call(
        kern,
        grid_spec=grid_spec,
        out_shape=jax.ShapeDtypeStruct((n_seq * t_new, ATT_WIDTH), F32),
        compiler_params=_params("parallel", "arbitrary"),
        name="fox_sample",
    )(page_table.reshape(-1), proj, k_new, v_new, qc_col, bias_new, pbias, g,
      *([cache_k] * pages_per_step), *([cache_v] * pages_per_step))


def _causal_conv(x, prev, w, b, rows_per_seq):
    n = x.shape[0]
    row = lax.broadcasted_iota(jnp.int32, x.shape, 0)
    out = None
    for j in range(CONV_W - 1, 0, -1):
        if rows_per_seq is None:
            shifted = jnp.where(row >= j, pltpu.roll(x, j, 0), pltpu.roll(prev, j, 0))
        else:
            shifted = jnp.where(lax.rem(row, rows_per_seq) >= j, pltpu.roll(x, j, 0),
                                pltpu.roll(prev, (j - rows_per_seq) % n, 0))
        tap = CONV_W - 1 - j
        term = shifted * w[tap:tap + 1, :]
        out = term if out is None else out + term
    out = out + x * w[CONV_W - 1:CONV_W, :]
    out = out + b
    return out * _sigmoid(out)


def _ssd_scalars(misc, dtb, alog, valid, tri, lsel):
    dtf = _softplus(misc + dtb)
    a = dtf * (-(jnp.exp(alog) * valid))
    acum = _select_rows(tri, a)
    if lsel is None:
        acum_end = acum[CHUNK - 1:CHUNK, :]
    else:
        acum_end = _select_rows(lsel, acum)
    return dtf, acum, acum_end


def _ssd_diag(xs, bm, cm, dtf, acum, visible):
    acum_t = acum.T
    dt_t = dtf.T
    lane = lax.broadcasted_iota(jnp.int32, (xs.shape[0], LANES), 1)
    heads_per_group = N_SSM_HEADS // N_SSM_GROUPS
    blocks = []
    for g in range(N_SSM_GROUPS):
        gs = slice(g * D_STATE, (g + 1) * D_STATE)
        cb = _dot_nt(cm[:, gs].astype(BF16), bm[:, gs].astype(BF16))
        for j in range(heads_per_group // 2):
            blk = g * (heads_per_group // 2) + j
            xpair = xs[:, blk * LANES:(blk + 1) * LANES]
            y = None
            for hh in range(2):
                ch = DT_LANE + 2 * blk + hh
                seg = acum[:, ch:ch + 1] - acum_t[ch:ch + 1, :]
                decay = jnp.exp(jnp.where(visible, seg, NEG))
                m = (cb * decay) * dt_t[ch:ch + 1, :]
                in_half = (lane >= SSM_HEADDIM) if hh else (lane < SSM_HEADDIM)
                xh = jnp.where(in_half, xpair, 0.0)
                part = _dot(m.astype(BF16), xh.astype(BF16))
                y = part if y is None else y + part
            blocks.append(y)
    return jnp.concatenate(blocks, axis=1)


def _ssd_prompt_kernel(z_ref, xs_ref, bc_ref, misc_ref, cw_ref, cb_ref, dtb_ref, alog_ref,
                       valid_ref, drow_ref, g_ref, fb_ref, tri_ref, e_ref,
                       o_ref, st_ref, logf_ref, cum_ref, cumt_ref,
                       ht_ref, pxs_ref, pbc_ref, fsum_ref, *, chunks_per_seq):
    c = lax.rem(pl.program_id(0), chunks_per_seq)

    @pl.when(c == 0)
    def _():
        ht_ref[...] = jnp.zeros_like(ht_ref)
        pxs_ref[...] = jnp.zeros_like(pxs_ref)
        pbc_ref[...] = jnp.zeros_like(pbc_ref)
        fsum_ref[...] = jnp.zeros_like(fsum_ref)

    logf, fsum = _forget_gates(misc_ref[...], fb_ref[...], tri_ref[...], fsum_ref[...])
    fsum_ref[...] = fsum[CHUNK - 1:CHUNK, :]
    logf_ref[...] = _head_lanes(logf)
    cum_ref[...] = _head_lanes(fsum)
    cumt_ref[...] = fsum.T[F_LANE:F_LANE + N_ATT_HEADS, :]

    xs_raw = xs_ref[...]
    bc_raw = bc_ref[...]
    cw = cw_ref[...]
    cb = cb_ref[...]
    xs = _causal_conv(xs_raw, pxs_ref[...], cw[:, :SSM_WIDTH], cb[:, :SSM_WIDTH], None)
    bc = _causal_conv(bc_raw, pbc_ref[...], cw[:, SSM_WIDTH:], cb[:, SSM_WIDTH:], None)
    pxs_ref[...] = xs_raw
    pbc_ref[...] = bc_raw
    bm = bc[:, :N_SSM_GROUPS * D_STATE]
    cm = bc[:, N_SSM_GROUPS * D_STATE:]

    dtf, acum, acum_end = _ssd_scalars(misc_ref[...], dtb_ref[...], alog_ref[...], valid_ref[...],
                                       tri_ref[...], None)
    row = lax.broadcasted_iota(jnp.int32, (CHUNK, CHUNK), 0)
    col = lax.broadcasted_iota(jnp.int32, (CHUNK, CHUNK), 1)
    y = _ssd_diag(xs, bm, cm, dtf, acum, col <= row)

    e = e_ref[...]
    exp_a = jnp.exp(_select_cols(acum, e))
    w_end = _select_cols(jnp.exp(acum_end - acum) * dtf, e)
    xw = xs * w_end
    ht = ht_ref[...]
    y_off = []
    new_states = []
    for g in range(N_SSM_GROUPS):
        gs = slice(g * D_STATE, (g + 1) * D_STATE)
        ws = slice(g * GROUP_WIDTH, (g + 1) * GROUP_WIDTH)
        y_off.append(_dot(cm[:, gs].astype(BF16), ht[:, ws].astype(BF16)))
        new_states.append(_dot(bm[:, gs].T.astype(BF16), xw[:, ws].astype(BF16)))
    y = y + jnp.concatenate(y_off, axis=1) * exp_a
    ht_new = exp_a[CHUNK - 1:CHUNK, :] * ht + jnp.concatenate(new_states, axis=1)
    ht_ref[...] = ht_new

    y = y + drow_ref[...] * xs
    z = z_ref[...]
    y = y * (z * _sigmoid(z))
    o_ref[...] = _rms(y, g_ref[...])

    @pl.when(c == chunks_per_seq - 1)
    def _():
        for j in range(SSM_WIDTH // LANES):
            st_ref[j * LANES:(j + 1) * LANES, :] = ht_new[:, j * LANES:(j + 1) * LANES].T


def _ssd_consts(l_rows_per_seq):
    r = lax.broadcasted_iota(jnp.int32, (CHUNK, CHUNK), 0)
    c = lax.broadcasted_iota(jnp.int32, (CHUNK, CHUNK), 1)
    same = (r // l_rows_per_seq) == (c // l_rows_per_seq)
    tri = (same & (c <= r)).astype(BF16)
    lsel = (c == (r // l_rows_per_seq) * l_rows_per_seq + l_rows_per_seq - 1).astype(BF16)
    return tri, lsel


def _expand_matrix():
    r = lax.broadcasted_iota(jnp.int32, (LANES, SSM_WIDTH), 0)
    c = lax.broadcasted_iota(jnp.int32, (LANES, SSM_WIDTH), 1)
    return (r == DT_LANE + c // SSM_HEADDIM).astype(BF16)


def _row_spec(shape):
    return pl.BlockSpec(shape, lambda i: (0, 0))


def _ssd_param_specs(layer):
    return [_layer_spec((CONV_W, CONV_DIM), layer), _layer_spec((1, CONV_DIM), layer),
            _layer_spec((1, LANES), layer), _layer_spec((1, LANES), layer), _row_spec((1, LANES)),
            _layer_spec((1, SSM_WIDTH), layer), _layer_spec((1, SSM_WIDTH), layer),
            _layer_spec((1, LANES), layer)]


def _ssd_prompt(proj, cw, cb, dtb, alog, valid, drow, g, fb, e, layer, batch, seq):
    chunks_per_seq = seq // CHUNK
    tri, _ = _ssd_consts(CHUNK)
    kern = functools.partial(_ssd_prompt_kernel, chunks_per_seq=chunks_per_seq)
    return pl.pallas_call(
        kern,
        grid=(batch * chunks_per_seq,),
        in_specs=[
            pl.BlockSpec((CHUNK, SSM_WIDTH), lambda i: (i, COL_Z // SSM_WIDTH)),
            pl.BlockSpec((CHUNK, SSM_WIDTH), lambda i: (i, COL_XS // SSM_WIDTH)),
            pl.BlockSpec((CHUNK, BC_WIDTH), lambda i: (i, COL_BC // BC_WIDTH)),
            pl.BlockSpec((CHUNK, LANES), lambda i: (i, COL_MISC // LANES)),
            *_ssd_param_specs(layer),
            _row_spec((CHUNK, CHUNK)), _row_spec((LANES, SSM_WIDTH)),
        ],
        out_specs=[
            pl.BlockSpec((CHUNK, SSM_WIDTH), lambda i: (i, 0)),
            pl.BlockSpec((None, SSM_WIDTH, D_STATE), lambda i: (i // chunks_per_seq, 0, 0)),
            pl.BlockSpec((CHUNK, N_ATT_HEADS), lambda i: (i, 0)),
            pl.BlockSpec((CHUNK, N_ATT_HEADS), lambda i: (i, 0)),
            pl.BlockSpec((N_ATT_HEADS, CHUNK), lambda i: (0, i)),
        ],
        out_shape=[
            jax.ShapeDtypeStruct((batch * seq, SSM_WIDTH), F32),
            jax.ShapeDtypeStruct((batch, SSM_WIDTH, D_STATE), F32),
            jax.ShapeDtypeStruct((batch * seq, N_ATT_HEADS), F32),
            jax.ShapeDtypeStruct((batch * seq, N_ATT_HEADS), F32),
            jax.ShapeDtypeStruct((N_ATT_HEADS, batch * seq), F32),
        ],
        scratch_shapes=[
            pltpu.VMEM((D_STATE, SSM_WIDTH), F32),
            pltpu.VMEM((CHUNK, SSM_WIDTH), F32),
            pltpu.VMEM((CHUNK, BC_WIDTH), F32),
            pltpu.VMEM((1, LANES), F32),
        ],
        compiler_params=_params("arbitrary"),
        name="ssd_prompt",
    )(proj, proj, proj, proj, cw, cb, dtb, alog, valid, drow, g, fb, tri, e)


def _ssd_sample_kernel(z_ref, xs_ref, bc_ref, misc_ref, prev_ref, h0_ref, cw_ref, cb_ref, dtb_ref,
                       alog_ref, valid_ref, drow_ref, g_ref, fb_ref, tri_ref, lsel_ref, e_ref,
                       o_ref, st_ref, logf_ref, cumt_ref, *, n_seq, t_new):
    n_rows = n_seq * t_new
    live = lax.broadcasted_iota(jnp.int32, (CHUNK, 1), 0) < n_rows
    xs_raw = jnp.where(live, xs_ref[...], 0.0)
    bc_raw = jnp.where(live, bc_ref[...], 0.0)
    z = jnp.where(live, z_ref[...], 0.0)
    misc = jnp.where(live, misc_ref[...], 0.0)
    logf, fsum = _forget_gates(misc, fb_ref[...], tri_ref[...], 0.0)
    logf_ref[...] = _head_lanes(logf)[0:n_rows, :]
    cumt_ref[...] = fsum.T[F_LANE:F_LANE + N_ATT_HEADS, :]
    cw = cw_ref[...]
    cb = cb_ref[...]
    prev = prev_ref[...]
    xs = _causal_conv(xs_raw, prev[:, :SSM_WIDTH], cw[:, :SSM_WIDTH], cb[:, :SSM_WIDTH], t_new)
    bc = _causal_conv(bc_raw, prev[:, SSM_WIDTH:], cw[:, SSM_WIDTH:], cb[:, SSM_WIDTH:], t_new)
    bm = bc[:, :N_SSM_GROUPS * D_STATE]
    cm = bc[:, N_SSM_GROUPS * D_STATE:]

    dtf, acum, acum_end = _ssd_scalars(misc, dtb_ref[...], alog_ref[...], valid_ref[...],
                                       tri_ref[...], lsel_ref[...])
    row = lax.broadcasted_iota(jnp.int32, (CHUNK, CHUNK), 0)
    col = lax.broadcasted_iota(jnp.int32, (CHUNK, CHUNK), 1)
    visible = (row // t_new == col // t_new) & (col <= row)
    y = _ssd_diag(xs, bm, cm, dtf, acum, visible)

    e = e_ref[...]
    exp_a = jnp.exp(_select_cols(acum, e))
    w_end = _select_cols(jnp.exp(acum_end - acum) * dtf, e)
    xw = xs * w_end
    bm_t = [bm[:, g * D_STATE:(g + 1) * D_STATE].T for g in range(N_SSM_GROUPS)]
    lane_seq = lax.broadcasted_iota(jnp.int32, (D_STATE, CHUNK), 1) // t_new

    y_off_rows = []
    for b in range(n_seq):
        rows = slice(b * t_new, (b + 1) * t_new)
        h0 = h0_ref[b]
        ht = jnp.concatenate(
            [h0[j * LANES:(j + 1) * LANES, :].T for j in range(SSM_WIDTH // LANES)], axis=1)
        y_off = []
        new_states = []
        for g in range(N_SSM_GROUPS):
            gs = slice(g * D_STATE, (g + 1) * D_STATE)
            ws = slice(g * GROUP_WIDTH, (g + 1) * GROUP_WIDTH)
            y_off.append(_dot(cm[:, gs].astype(BF16), ht[:, ws].astype(BF16))[rows, :])
            bt = jnp.where(lane_seq == b, bm_t[g], 0.0)
            new_states.append(_dot(bt.astype(BF16), xw[:, ws].astype(BF16)))
        y_off_rows.append(jnp.concatenate(y_off, axis=1))
        last = (b + 1) * t_new - 1
        ht_new = exp_a[last:last + 1, :] * ht + jnp.concatenate(new_states, axis=1)
        for j in range(SSM_WIDTH // LANES):
            st_ref[b, j * LANES:(j + 1) * LANES, :] = ht_new[:, j * LANES:(j + 1) * LANES].T
    y_off_rows.append(jnp.zeros((CHUNK - n_rows, SSM_WIDTH), F32))
    y = y + jnp.concatenate(y_off_rows, axis=0) * exp_a

    y = y + drow_ref[...] * xs
    y = y * (z * _sigmoid(z))
    o_ref[...] = _rms(y, g_ref[...])[0:n_rows, :]


def _ssd_sample(proj, prev, h0, cw, cb, dtb, alog, valid, drow, g, fb, e, layer, row0, n_seq, t_new):
    tri, lsel = _ssd_consts(t_new)
    rb = row0 // CHUNK
    kern = functools.partial(_ssd_sample_kernel, n_seq=n_seq, t_new=t_new)
    full3 = pl.BlockSpec((n_seq, SSM_WIDTH, D_STATE), lambda i: (0, 0, 0))
    return pl.pallas_call(
        kern,
        grid=(1,),
        in_specs=[
            pl.BlockSpec((CHUNK, SSM_WIDTH), lambda i: (rb, COL_Z // SSM_WIDTH)),
            pl.BlockSpec((CHUNK, SSM_WIDTH), lambda i: (rb, COL_XS // SSM_WIDTH)),
            pl.BlockSpec((CHUNK, BC_WIDTH), lambda i: (rb, COL_BC // BC_WIDTH)),
            pl.BlockSpec((CHUNK, LANES), lambda i: (rb, COL_MISC // LANES)),
            _layer_spec((CHUNK, CONV_DIM), layer),
            _layer_spec((n_seq, SSM_WIDTH, D_STATE), layer),
            *_ssd_param_specs(layer),
            _row_spec((CHUNK, CHUNK)), _row_spec((CHUNK, CHUNK)), _row_spec((LANES, SSM_WIDTH)),
        ],
        out_specs=[
            pl.BlockSpec((n_seq * t_new, SSM_WIDTH), lambda i: (0, 0)),
            full3,
            pl.BlockSpec((n_seq * t_new, N_ATT_HEADS), lambda i: (0, 0)),
            pl.BlockSpec((N_ATT_HEADS, CHUNK), lambda i: (0, 0)),
        ],
        out_shape=[
            jax.ShapeDtypeStruct((n_seq * t_new, SSM_WIDTH), F32),
            jax.ShapeDtypeStruct((n_seq, SSM_WIDTH, D_STATE), F32),
            jax.ShapeDtypeStruct((n_seq * t_new, N_ATT_HEADS), F32),
            jax.ShapeDtypeStruct((N_ATT_HEADS, CHUNK), F32),
        ],
        compiler_params=_params("arbitrary"),
        name="ssd_sample",
    )(proj, proj, proj, proj, prev, h0, cw, cb, dtb, alog, valid, drow, g, fb, tri, lsel, e)


def _out_proj_kernel(h_ref, ap_ref, sp_ref, as_ref, ss_ref, wa_ref, ws_ref, o_ref, *, n_prompt_tiles):
    i = pl.program_id(0)

    def mix(att, ssm):
        o_ref[...] = h_ref[...] + (_dot(att.astype(BF16), wa_ref[...]) + _dot(ssm.astype(BF16), ws_ref[...]))

    @pl.when(i < n_prompt_tiles)
    def _():
        mix(ap_ref[...], sp_ref[...])

    @pl.when(i >= n_prompt_tiles)
    def _():
        pad = jnp.zeros((h_ref.shape[0] - as_ref.shape[0], as_ref.shape[1]), F32)
        mix(jnp.concatenate([as_ref[...], pad], axis=0), jnp.concatenate([ss_ref[...], pad], axis=0))


def _out_proj(h, o_att_p, o_ssm_p, o_att_s, o_ssm_s, w, layer):
    t = h.shape[0]
    tm = ROW_ALIGN
    n_prompt, n_sample = o_att_p.shape[0], o_att_s.shape[0]
    n_prompt_tiles = n_prompt // tm
    assert n_prompt % tm == 0 and t // tm == n_prompt_tiles + 1 and n_sample <= tm
    prompt_spec = pl.BlockSpec((tm, ATT_WIDTH), lambda i: (jnp.minimum(i, n_prompt_tiles - 1), 0))
    sample_spec = pl.BlockSpec((n_sample, ATT_WIDTH), lambda i: (0, 0))
    return pl.pallas_call(
        functools.partial(_out_proj_kernel, n_prompt_tiles=n_prompt_tiles),
        grid=(t // tm,),
        in_specs=[
            pl.BlockSpec((tm, D_MODEL), lambda i: (i, 0)),
            prompt_spec, prompt_spec, sample_spec, sample_spec,
            pl.BlockSpec((None, ATT_WIDTH, D_MODEL), lambda i: (layer, 0, 0)),
            pl.BlockSpec((None, SSM_WIDTH, D_MODEL), lambda i: (layer, 1, 0)),
        ],
        out_specs=pl.BlockSpec((tm, D_MODEL), lambda i: (i, 0)),
        out_shape=jax.ShapeDtypeStruct((t, D_MODEL), F32),
        compiler_params=_params("arbitrary"),
        name="out_proj",
    )(h, o_att_p, o_ssm_p, o_att_s, o_ssm_s, w, w)


def _mlp_kernel(h_ref, g_ref, wu_ref, wd_ref, o_ref, hn_ref):
    @pl.when(pl.program_id(1) == 0)
    def _():
        h = h_ref[...]
        hn_ref[...] = _rms(h, g_ref[...]).astype(BF16)
        o_ref[...] = h

    u = jnp.maximum(_dot(hn_ref[...], wu_ref[...]), 0.0)
    o_ref[...] += _dot((u * u).astype(BF16), wd_ref[...])


def _mlp(h, g, wu, wd, layer):
    t = h.shape[0]
    tm = TOKEN_TILE
    tf = 1024
    return pl.pallas_call(
        _mlp_kernel,
        grid=(t // tm, D_FF // tf),
        in_specs=[
            pl.BlockSpec((tm, D_MODEL), lambda i, f: (i, 0)),
            _layer_spec((1, D_MODEL), layer),
            pl.BlockSpec((None, D_MODEL, tf), lambda i, f: (layer, 0, f)),
            pl.BlockSpec((None, tf, D_MODEL), lambda i, f: (layer, f, 0)),
        ],
        out_specs=pl.BlockSpec((tm, D_MODEL), lambda i, f: (i, 0)),
        out_shape=jax.ShapeDtypeStruct((t, D_MODEL), F32),
        scratch_shapes=[pltpu.VMEM((tm, D_MODEL), BF16)],
        compiler_params=_params("parallel", "arbitrary"),
        name="mlp",
    )(h, g, wu, wd)


def _ple_update(h_ref, p_ref, g_ref, wg_ref, wp_ref):
    h = h_ref[...]
    gate = _sigmoid(_dot(_rms(h, g_ref[...]).astype(BF16), wg_ref[...]))
    return h + _dot(p_ref[...].astype(BF16), wp_ref[...]) * gate


def _ple_kernel(h_ref, p_ref, g_ref, wg_ref, wp_ref, o_ref):
    o_ref[...] = _ple_update(h_ref, p_ref, g_ref, wg_ref, wp_ref)


def _ple_final_kernel(h_ref, p_ref, g_ref, wg_ref, wp_ref, gf_ref, yp_ref, ys_ref, *, n_prompt_tiles):
    y = _rms(_ple_update(h_ref, p_ref, g_ref, wg_ref, wp_ref), gf_ref[...])
    i = pl.program_id(0)

    @pl.when(i < n_prompt_tiles)
    def _():
        yp_ref[...] = y

    @pl.when(i >= n_prompt_tiles)
    def _():
        ys_ref[...] = y[0:ys_ref.shape[0], :]


def _ple(h, p, g, wg, wp, layer, g_final=None, n_prompt=None, n_sample=None):
    t = h.shape[0]
    tm = ROW_ALIGN
    ple_dim = p.shape[-1]
    row = lambda: pl.BlockSpec((1, D_MODEL), lambda i: (0, 0))
    in_specs = [
        pl.BlockSpec((tm, D_MODEL), lambda i: (i, 0)),
        pl.BlockSpec((None, tm, ple_dim), lambda i: (layer, i, 0)),
        _layer_spec((1, D_MODEL), layer),
        pl.BlockSpec((None, D_MODEL, D_MODEL), lambda i: (layer, 0, 0)),
        pl.BlockSpec((None, ple_dim, D_MODEL), lambda i: (layer, 0, 0)),
    ]
    if g_final is None:
        return pl.pallas_call(
            _ple_kernel,
            grid=(t // tm,),
            in_specs=in_specs,
            out_specs=pl.BlockSpec((tm, D_MODEL), lambda i: (i, 0)),
            out_shape=jax.ShapeDtypeStruct((t, D_MODEL), F32),
            compiler_params=_params("parallel"),
            name="ple",
        )(h, p, g, wg, wp)
    n_prompt_tiles = n_prompt // tm
    assert n_prompt % tm == 0 and t // tm == n_prompt_tiles + 1 and n_sample <= tm
    return pl.pallas_call(
        functools.partial(_ple_final_kernel, n_prompt_tiles=n_prompt_tiles),
        grid=(t // tm,),
        in_specs=in_specs + [row()],
        out_specs=[
            pl.BlockSpec((tm, D_MODEL), lambda i: (jnp.minimum(i, n_prompt_tiles - 1), 0)),
            pl.BlockSpec((n_sample, D_MODEL), lambda i: (0, 0)),
        ],
        out_shape=[jax.ShapeDtypeStruct((n_prompt, D_MODEL), F32),
                   jax.ShapeDtypeStruct((n_sample, D_MODEL), F32)],
        compiler_params=_params("arbitrary"),
        name="ple_final",
    )(h, p, g, wg, wp, g_final)


def _stack_rows_kernel(*refs, n_layers):
    k_refs, v_refs = refs[:n_layers], refs[n_layers:2 * n_layers]
    ko_ref, vo_ref = refs[2 * n_layers:]
    layer = pl.program_id(0)
    for n in range(n_layers):
        @pl.when(layer == n)
        def _():
            ko_ref[...] = k_refs[n][...]
            vo_ref[...] = v_refs[n][...]


def _stack_rows(k_layers, v_layers, n_rows):
    n_layers = len(k_layers)
    rows = 2 * ROW_ALIGN
    assert n_rows % rows == 0
    n_tiles = n_rows // rows
    tail = k_layers[0].shape[1:]
    zeros = (0,) * len(tail)
    out = jax.ShapeDtypeStruct((n_layers, n_rows) + tail, k_layers[0].dtype)

    def src_spec(n):
        return pl.BlockSpec((rows,) + tail,
                            lambda l, i: (jnp.where(l == n, i, jnp.where(l > n, n_tiles - 1, 0)),) + zeros)

    out_spec = pl.BlockSpec((None, rows) + tail, lambda l, i: (l, i) + zeros)
    return pl.pallas_call(
        functools.partial(_stack_rows_kernel, n_layers=n_layers),
        grid=(n_layers, n_tiles),
        in_specs=[src_spec(n) for n in range(n_layers)] * 2,
        out_specs=[out_spec, out_spec],
        out_shape=[out, out],
        compiler_params=_params("arbitrary", "arbitrary"),
        name="stack_kv",
    )(*k_layers, *v_layers)


def _misc_rows(values, lane0):
    n = values.shape[1]
    return jnp.pad(values.astype(F32), ((0, 0), (lane0, LANES - lane0 - n)))[:, None, :]


def _rows(x):
    return x.astype(F32)[:, None, :]


def kernel(x_prompt, x_sample, cache_k, cache_v, cache_logf, state_ssm, state_conv, page_table, p_prompt, p_sample, norm_mix, w_in, b_forget, conv_w, conv_b, dt_bias, a_log, d_skip, norm_attn_out, norm_ssm_out, w_out, norm_mlp, w_up, w_down, norm_ple, w_ple_gate, w_ple_proj, norm_final):
    batch, seq, _ = x_prompt.shape
    n_seq, t_new, _ = x_sample.shape
    depth = w_in.shape[0]
    n_prompt = batch * seq
    n_sample = n_seq * t_new
    t_pad = -(-(n_prompt + CHUNK) // TOKEN_TILE) * TOKEN_TILE
    n_tail = t_pad - n_prompt - n_sample
    n_phys, page = cache_k.shape[1], cache_k.shape[2]

    h = jnp.concatenate([x_prompt.reshape(n_prompt, D_MODEL), x_sample.reshape(n_sample, D_MODEL),
                         jnp.zeros((n_tail, D_MODEL), F32)], axis=0)
    ple_dim = p_prompt.shape[-1]
    p_all = jnp.concatenate([p_prompt.reshape(depth, n_prompt, ple_dim),
                             p_sample.reshape(depth, n_sample, ple_dim),
                             jnp.zeros((depth, n_tail, ple_dim), F32)], axis=1)
    w_in_r = _reorder_w_in(w_in)
    w_out_b = w_out.astype(BF16)
    w_up_b = w_up.astype(BF16)
    w_down_b = w_down.astype(BF16)
    w_gate_b = w_ple_gate.astype(BF16)
    w_proj_b = w_ple_proj.astype(BF16)

    expand = _expand_matrix()
    valid_row = _misc_rows(jnp.ones((1, N_SSM_HEADS), F32), DT_LANE)[0]
    ssm_params = (conv_w, _rows(conv_b), _misc_rows(dt_bias, DT_LANE), _misc_rows(a_log, DT_LANE),
                  valid_row, _rows(jnp.repeat(d_skip, SSM_HEADDIM, axis=1)), _rows(norm_ssm_out),
                  _misc_rows(b_forget, F_LANE), expand)
    g_mix, g_att, g_mlp, g_ple = _rows(norm_mix), _rows(norm_attn_out), _rows(norm_mlp), _rows(norm_ple)
    prev = jnp.pad(state_conv, ((0, 0), (0, 0), (t_new - (CONV_W - 1), 0), (0, 0)))
    prev = jnp.pad(prev.reshape(depth, n_sample, CONV_DIM), ((0, 0), (0, CHUNK - n_sample), (0, 0)))
    h0 = state_ssm.reshape(depth, n_seq, SSM_WIDTH, D_STATE)
    pbias = _past_bias(cache_logf, page_table)
    causal_new = jnp.arange(t_new)[None, :] <= jnp.arange(t_new)[:, None]
    same_head = jnp.eye(N_ATT_HEADS, dtype=bool)

    outs = {name: [] for name in ("kp", "vp", "fp", "sp", "cp", "ks", "vs", "fs", "ss", "cs")}
    n_new = t_new * N_ATT_HEADS
    for l in range(depth):
        proj, k_rows, v_rows, k_new, v_new = _in_proj(h, g_mix, w_in_r, l, n_prompt, n_sample)
        o_ssm_p, st_p, logf_p, cum, cumt = _ssd_prompt(proj, *ssm_params, l, batch, seq)
        o_ssm_s, st_s, logf_s, cumt_s = _ssd_sample(proj, prev, h0, *ssm_params, l, n_prompt, n_seq, t_new)

        o_att_p = _fox_prompt(proj, cum, cumt, g_att, l, batch, seq)
        qc = cumt_s[:, :n_sample].reshape(N_ATT_HEADS, n_seq, t_new)
        qc = jnp.transpose(qc, (1, 0, 2))
        qc_col = qc.reshape(n_seq, N_ATT_HEADS * t_new, 1)
        bias_new = jnp.where(causal_new[None, None, :, :, None] & same_head[None, :, None, None, :],
                             -qc[:, :, None, :, None], NEG)
        bias_new = jnp.concatenate(
            [bias_new.reshape(n_seq, N_ATT_HEADS * t_new, t_new * N_ATT_HEADS),
             jnp.full((n_seq, N_ATT_HEADS * t_new, LANES - t_new * N_ATT_HEADS), NEG, F32)], axis=2)
        o_att_s = _fox_sample(proj, k_new.reshape(n_seq, n_new, ATT_HEAD_DIM),
                              v_new.reshape(n_seq, n_new, ATT_HEAD_DIM), qc_col, bias_new,
                              pbias, cache_k, cache_v, page_table, g_att, l, n_prompt, t_new)

        h = _out_proj(h, o_att_p, o_ssm_p, o_att_s, o_ssm_s, w_out_b, l)
        h = _mlp(h, g_mlp, w_up_b, w_down_b, l)
        ple_args = (h, p_all, g_ple, w_gate_b, w_proj_b, l)
        if l < depth - 1:
            h = _ple(*ple_args)
        else:
            y_prompt, y_sample = _ple(*ple_args, norm_final[None, :], n_prompt, n_sample)

        xbc = slice(COL_XS, COL_XS + CONV_DIM)
        outs["kp"].append(k_rows)
        outs["vp"].append(v_rows)
        outs["ks"].append(k_new)
        outs["vs"].append(v_new)
        outs["fp"].append(logf_p.reshape(batch, seq, N_ATT_HEADS))
        outs["sp"].append(st_p.reshape(batch, N_SSM_HEADS, SSM_HEADDIM, D_STATE))
        outs["cp"].append(jnp.stack(
            [proj[(b + 1) * seq - (CONV_W - 1):(b + 1) * seq, xbc] for b in range(batch)]))
        outs["fs"].append(logf_s.reshape(n_seq, t_new, N_ATT_HEADS))
        outs["ss"].append(st_s.reshape(n_seq, N_SSM_HEADS, SSM_HEADDIM, D_STATE))
        outs["cs"].append(proj[n_prompt:n_prompt + n_sample].reshape(n_seq, t_new, PROJ_W)
                          [:, t_new - (CONV_W - 1):, xbc])

    kv_shape_p = (depth, batch, seq, N_ATT_HEADS, ATT_HEAD_DIM)
    kv_shape_s = (depth, n_seq, t_new, N_ATT_HEADS, ATT_HEAD_DIM)
    stack = lambda name: jnp.stack(outs[name])
    k_prompt, v_prompt = _stack_rows(outs["kp"], outs["vp"], n_prompt)
    return (y_prompt.reshape(batch, seq, D_MODEL), y_sample.reshape(n_seq, t_new, D_MODEL),
            k_prompt.reshape(kv_shape_p), v_prompt.reshape(kv_shape_p),
            stack("fp"), stack("sp"), stack("cp"),
            stack("ks").reshape(kv_shape_s), stack("vs").reshape(kv_shape_s),
            stack("fs"), stack("ss"), stack("cs"))
```

```python
import functools

import numpy as np
import jax
import jax.numpy as jnp
from jax import lax
from jax.experimental import pallas as pl
from jax.experimental.pallas import tpu as pltpu

F32 = jnp.float32
BF16 = jnp.bfloat16

D_MODEL = 2048
ATT_HEAD_DIM = 128
ATT_WIDTH = D_MODEL // 2
N_ATT_HEADS = ATT_WIDTH // ATT_HEAD_DIM
ATT_SCALE = ATT_HEAD_DIM ** -0.5
SSM_WIDTH = D_MODEL - ATT_WIDTH
SSM_HEADDIM = 64
N_SSM_HEADS = SSM_WIDTH // SSM_HEADDIM
N_SSM_GROUPS = 2
GROUP_WIDTH = SSM_WIDTH // N_SSM_GROUPS
D_STATE = 128
CONV_W = 4
BC_WIDTH = 2 * N_SSM_GROUPS * D_STATE
CONV_DIM = SSM_WIDTH + BC_WIDTH
D_FF = 4 * D_MODEL
RMS_EPS = 1e-6
NEG = -1e30

SUBLANES = 8
LANES = 128
VMEM_LIMIT = 56 * 1024 * 1024

PROJ_TILE = ATT_WIDTH
KV_TILES = 2
PROJ_W = 6 * PROJ_TILE
COL_Q = 0
COL_Z = ATT_WIDTH
COL_XS = COL_Z + SSM_WIDTH
COL_BC = COL_XS + SSM_WIDTH
COL_MISC = COL_BC + BC_WIDTH
PROJ_OUT_W = PROJ_W - KV_TILES * PROJ_TILE
F_LANE = 0
DT_LANE = N_ATT_HEADS

CHUNK = 128
TOKEN_TILE = 768
ROW_ALIGN = 256


def _rms(x, g):
    ms = jnp.mean(x * x, axis=-1, keepdims=True)
    return (x * lax.rsqrt(ms + RMS_EPS)) * g


def _sigmoid(x):
    return 1.0 / (1.0 + jnp.exp(-x))


def _softplus(x):
    return jnp.maximum(x, 0.0) + jnp.log1p(jnp.exp(-jnp.abs(x)))


def _dot(a, b):
    return jnp.dot(a, b, preferred_element_type=F32)


def _bf16_terms(x):
    hi = x.astype(BF16)
    rest = x - hi.astype(F32)
    mid = rest.astype(BF16)
    lo = (rest - mid.astype(F32)).astype(BF16)
    return hi, mid, lo


def _select_rows(sel, x):
    return functools.reduce(lambda a, b: a + b, [_dot(sel, t) for t in _bf16_terms(x)])


def _select_cols(x, sel):
    return functools.reduce(lambda a, b: a + b, [_dot(t, sel) for t in _bf16_terms(x)])


def _dot_nt(a, b):
    return lax.dot_general(a, b, (((1,), (1,)), ((), ())), preferred_element_type=F32)


def _tile_lanes(x, rep):
    return x if rep == 1 else jnp.concatenate([x] * rep, axis=1)


def _layer_spec(shape, layer):
    return pl.BlockSpec((None,) + shape, lambda *_: (layer,) + (0,) * len(shape))


def _params(*semantics):
    return pltpu.CompilerParams(dimension_semantics=semantics, vmem_limit_bytes=VMEM_LIMIT)


IN_F = 3 * ATT_WIDTH
IN_Z = IN_F + N_ATT_HEADS
IN_DT = IN_Z + SSM_WIDTH + CONV_DIM
IN_PROJ_DIM = IN_DT + N_SSM_HEADS


IN_LAST = IN_PROJ_DIM - PROJ_TILE


def _reorder_w_in_kernel(w_ref, f_ref, o_ref):
    j = pl.program_id(1)

    @pl.when(j < PROJ_W // PROJ_TILE - 1)
    def _():
        o_ref[...] = w_ref[0].T.astype(BF16)

    @pl.when(j == PROJ_W // PROJ_TILE - 1)
    def _():
        w = w_ref[0]
        bc0 = IN_DT - BC_WIDTH - IN_LAST
        pad = jnp.zeros((PROJ_TILE - BC_WIDTH - N_ATT_HEADS - N_SSM_HEADS, w.shape[1]), F32)
        rows = jnp.concatenate([w[bc0:bc0 + BC_WIDTH], f_ref[0], w[bc0 + BC_WIDTH:], pad], axis=0)
        o_ref[...] = rows.T.astype(BF16)


def _reorder_w_in(w_in):
    depth = w_in.shape[0]
    assert w_in.shape[2] == IN_PROJ_DIM
    n_tiles = PROJ_W // PROJ_TILE

    starts = (ATT_WIDTH, 2 * ATT_WIDTH, 0, IN_Z, IN_Z + SSM_WIDTH, IN_LAST)
    assert len(starts) == n_tiles

    def first_row(j):
        row = functools.reduce(lambda acc, t: jnp.where(j == t, starts[t], acc), range(1, n_tiles), starts[0])
        return pl.multiple_of(row, SUBLANES)

    return pl.pallas_call(
        _reorder_w_in_kernel,
        grid=(depth, n_tiles),
        in_specs=[
            pl.BlockSpec((pl.Element(1), pl.Element(PROJ_TILE), pl.Element(D_MODEL)),
                         lambda l, j: (l, first_row(j), 0)),
            pl.BlockSpec((pl.Element(1), pl.Element(N_ATT_HEADS), pl.Element(D_MODEL)),
                         lambda l, j: (l, IN_F, 0)),
        ],
        out_specs=pl.BlockSpec((None, D_MODEL, PROJ_TILE), lambda l, j: (l, 0, j)),
        out_shape=jax.ShapeDtypeStruct((depth, D_MODEL, PROJ_W), BF16),
        compiler_params=_params("parallel", "parallel"),
        name="w_in_reorder",
    )(jnp.swapaxes(w_in, 1, 2), jnp.swapaxes(w_in, 1, 2))


def _in_proj_kernel(x_ref, g_ref, w_ref, *rest, n_prompt, n_sample):
    o_ref, kv_ref, kp_ref, vp_ref, ks_ref, vs_ref, xn_ref = rest
    i = pl.program_id(0)
    j = pl.program_id(1)
    tm = x_ref.shape[0]

    @pl.when(j == 0)
    def _():
        xn_ref[...] = _rms(x_ref[...], g_ref[...]).astype(BF16)

    res = _dot(xn_ref[...], w_ref[...])

    def emit(prompt_ref, sample_ref):
        kv_ref[...] = res.astype(BF16)
        prompt_ref[...] = res.reshape(tm, N_ATT_HEADS, ATT_HEAD_DIM)

        @pl.when(i == n_prompt // tm)
        def _():
            r0 = n_prompt % tm
            sample_ref[...] = res[r0:r0 + n_sample, :].reshape(n_sample, N_ATT_HEADS, ATT_HEAD_DIM)

    @pl.when(j == 0)
    def _():
        emit(kp_ref, ks_ref)

    @pl.when(j == 1)
    def _():
        emit(vp_ref, vs_ref)

    @pl.when(j >= KV_TILES)
    def _():
        o_ref[...] = res


def _in_proj(h, g, w, layer, n_prompt, n_sample):
    t = h.shape[0]
    tm = TOKEN_TILE
    assert n_prompt % tm + n_sample <= tm
    head_shape = (N_ATT_HEADS, ATT_HEAD_DIM)
    cache_shapes = [(t,) + head_shape] * 2 + [(n_sample,) + head_shape] * 2
    cache_specs = [pl.BlockSpec((tm,) + head_shape, lambda i, j: (i, 0, 0))] * 2 + \
                  [pl.BlockSpec((n_sample,) + head_shape, lambda i, j: (0, 0, 0))] * 2
    return pl.pallas_call(
        functools.partial(_in_proj_kernel, n_prompt=n_prompt, n_sample=n_sample),
        grid=(t // tm, PROJ_W // PROJ_TILE),
        in_specs=[
            pl.BlockSpec((tm, D_MODEL), lambda i, j: (i, 0)),
            _layer_spec((1, D_MODEL), layer),
            pl.BlockSpec((None, D_MODEL, PROJ_TILE), lambda i, j: (layer, 0, j)),
        ],
        out_specs=[pl.BlockSpec((tm, PROJ_TILE), lambda i, j: (i, jnp.maximum(j - KV_TILES, 0))),
                   pl.BlockSpec((tm, PROJ_TILE), lambda i, j: (i, jnp.minimum(j, KV_TILES - 1)))]
                  + cache_specs,
        out_shape=[jax.ShapeDtypeStruct((t, PROJ_OUT_W), F32),
                   jax.ShapeDtypeStruct((t, KV_TILES * PROJ_TILE), BF16)]
                  + [jax.ShapeDtypeStruct(s, F32) for s in cache_shapes],
        scratch_shapes=[pltpu.VMEM((tm, D_MODEL), BF16)],
        compiler_params=_params("arbitrary", "arbitrary"),
        name="in_proj",
    )(h, g, w)


def _forget_gates(misc, fb, tri, carry):
    x = misc + fb
    lf = jnp.minimum(x, 0.0) - jnp.log1p(jnp.exp(-jnp.abs(x)))
    return lf, _select_rows(tri, lf) + carry


def _head_lanes(x):
    return x[:, F_LANE:F_LANE + N_ATT_HEADS]


def _fox_prompt_kernel(qi_ref, ki_ref, q_ref, k_ref, v_ref, cq_ref, ck_ref, g_ref, o_ref,
                       m_ref, l_ref, acc_ref, cqb_ref, *, tq, tk):
    p = pl.program_id(1)
    qi = qi_ref[p]
    ki = ki_ref[p]
    rep = tk // LANES

    @pl.when(ki == 0)
    def _():
        m_ref[...] = jnp.full_like(m_ref, NEG)
        l_ref[...] = jnp.zeros_like(l_ref)
        acc_ref[...] = jnp.zeros_like(acc_ref)
        cq = cq_ref[...]
        for h in range(N_ATT_HEADS):
            cqb_ref[h] = jnp.broadcast_to(cq[:, h:h + 1], (tq, LANES))

    def step(on_diagonal):
        ck = ck_ref[...]
        if on_diagonal:
            visible = (lax.broadcasted_iota(jnp.int32, (tq, tk), 1)
                       <= lax.broadcasted_iota(jnp.int32, (tq, tk), 0))
        def scores(h):
            sl = slice(h * ATT_HEAD_DIM, (h + 1) * ATT_HEAD_DIM)
            return _dot_nt(q_ref[:, sl].astype(BF16), k_ref[:, sl].astype(BF16))

        s_next = scores(0)
        for h in range(N_ATT_HEADS):
            sl = slice(h * ATT_HEAD_DIM, (h + 1) * ATT_HEAD_DIM)
            s = s_next
            if h + 1 < N_ATT_HEADS:
                s_next = scores(h + 1)
            s = s * ATT_SCALE + (_tile_lanes(cqb_ref[h], rep) - ck[h:h + 1, :])
            if on_diagonal:
                s = jnp.where(visible, s, NEG)
            m_prev = m_ref[h]
            m_new = jnp.maximum(m_prev, jnp.max(s, axis=-1, keepdims=True))
            alpha = jnp.exp(m_prev - m_new)
            pm = jnp.exp(s - _tile_lanes(m_new, rep))
            l_ref[h] = alpha * l_ref[h] + jnp.sum(pm, axis=-1, keepdims=True)
            acc_ref[:, sl] = alpha * acc_ref[:, sl] + _dot(pm.astype(BF16), v_ref[:, sl].astype(BF16))
            m_ref[h] = m_new

    @pl.when(ki < qi)
    def _():
        step(False)

    @pl.when(ki == qi)
    def _():
        step(True)
        for h in range(N_ATT_HEADS):
            sl = slice(h * ATT_HEAD_DIM, (h + 1) * ATT_HEAD_DIM)
            acc_ref[:, sl] = acc_ref[:, sl] / l_ref[h]
        o_ref[...] = _rms(acc_ref[...], g_ref[...])


def _fox_prompt(proj, kv, cum, cumt, g, layer, batch, seq):
    tq = tk = min(512, seq)
    nq = seq // tq
    pairs = [(a, b) for a in range(nq) for b in range(a + 1)]
    qi_tab = jnp.asarray(np.array([a for a, _ in pairs], np.int32))
    ki_tab = jnp.asarray(np.array([b for _, b in pairs], np.int32))
    kern = functools.partial(_fox_prompt_kernel, tq=tq, tk=tk)
    grid_spec = pltpu.PrefetchScalarGridSpec(
        num_scalar_prefetch=2,
        grid=(batch, len(pairs)),
        in_specs=[
            pl.BlockSpec((tq, ATT_WIDTH), lambda b, p, qt, kt: (b * nq + qt[p], COL_Q // ATT_WIDTH)),
            pl.BlockSpec((tk, ATT_WIDTH), lambda b, p, qt, kt: (b * nq + kt[p], 0)),
            pl.BlockSpec((tk, ATT_WIDTH), lambda b, p, qt, kt: (b * nq + kt[p], 1)),
            pl.BlockSpec((tq, N_ATT_HEADS), lambda b, p, qt, kt: (b * nq + qt[p], 0)),
            pl.BlockSpec((N_ATT_HEADS, tk), lambda b, p, qt, kt: (0, b * nq + kt[p])),
            _layer_spec((1, ATT_WIDTH), layer),
        ],
        out_specs=pl.BlockSpec((tq, ATT_WIDTH), lambda b, p, qt, kt: (b * nq + qt[p], 0)),
        scratch_shapes=[
            pltpu.VMEM((N_ATT_HEADS, tq, LANES), F32),
            pltpu.VMEM((N_ATT_HEADS, tq, LANES), F32),
            pltpu.VMEM((tq, ATT_WIDTH), F32),
            pltpu.VMEM((N_ATT_HEADS, tq, LANES), F32),
        ],
    )
    return pl.pallas_call(
        kern,
        grid_spec=grid_spec,
        out_shape=jax.ShapeDtypeStruct((batch * seq, ATT_WIDTH), F32),
        compiler_params=_params("parallel", "arbitrary"),
        name="fox_prompt",
    )(qi_tab, ki_tab, proj, kv, kv, cum, cumt, g)


def _past_bias_kernel(pt_ref, logf_hbm, m1_ref, s_ref, su_ref, e8_ref, o_ref, x_ref, sem, *,
                      n_pages):
    l = pl.program_id(0)
    b = pl.program_id(1)

    def page_copy(p):
        page = pt_ref[b * n_pages + p]
        return pltpu.make_async_copy(logf_hbm.at[l, pl.ds(page, 1), :], x_ref.at[pl.ds(p, 1), :], sem)

    def start(p, carry):
        page_copy(p).start()
        return carry

    def wait(p, carry):
        page_copy(p).wait()
        return carry

    lax.fori_loop(0, n_pages, start, 0)
    lax.fori_loop(0, n_pages, wait, 0)
    x = x_ref[...]
    within = _select_cols(x, m1_ref[...])
    totals = _select_cols(x, s_ref[...])
    later = _select_rows(su_ref[...], totals)
    o_ref[...] = within + _select_cols(later, e8_ref[...])


def _past_bias(cache_logf, page_table):
    depth, n_phys, page, heads = cache_logf.shape
    n_seq, n_pages = page_table.shape
    width = page * heads
    r = lax.broadcasted_iota(jnp.int32, (width, width), 0)
    c = lax.broadcasted_iota(jnp.int32, (width, width), 1)
    m1 = ((r % heads == c % heads) & (r // heads > c // heads)).astype(BF16)
    r = lax.broadcasted_iota(jnp.int32, (width, LANES), 0)
    c = lax.broadcasted_iota(jnp.int32, (width, LANES), 1)
    s = (r % heads == c).astype(BF16)
    r = lax.broadcasted_iota(jnp.int32, (n_pages, n_pages), 0)
    c = lax.broadcasted_iota(jnp.int32, (n_pages, n_pages), 1)
    su = (c > r).astype(BF16)
    r = lax.broadcasted_iota(jnp.int32, (LANES, width), 0)
    c = lax.broadcasted_iota(jnp.int32, (LANES, width), 1)
    e8 = (r == c % heads).astype(BF16)
    const = lambda shape: pl.BlockSpec(shape, lambda l, b, pt: (0, 0))
    grid_spec = pltpu.PrefetchScalarGridSpec(
        num_scalar_prefetch=1,
        grid=(depth, n_seq),
        in_specs=[
            pl.BlockSpec(memory_space=pl.ANY),
            const((width, width)), const((width, LANES)), const((n_pages, n_pages)),
            const((LANES, width)),
        ],
        out_specs=pl.BlockSpec((None, None, n_pages, width), lambda l, b, pt: (l, b, 0, 0)),
        scratch_shapes=[pltpu.VMEM((n_pages, width), F32), pltpu.SemaphoreType.DMA(())],
    )
    out = pl.pallas_call(
        functools.partial(_past_bias_kernel, n_pages=n_pages),
        grid_spec=grid_spec,
        out_shape=jax.ShapeDtypeStruct((depth, n_seq, n_pages, width), F32),
        compiler_params=_params("arbitrary", "arbitrary"),
        name="past_bias",
    )(page_table.reshape(-1), cache_logf.reshape(depth, n_phys, width), m1, s, su, e8)
    return out


def _fox_sample_kernel(pt_ref, q_ref, kn_ref, vn_ref, qc_ref, bn_ref, pb_ref, g_ref, *rest,
                       pages_per_step, t_new):
    k_refs = rest[:pages_per_step]
    v_refs = rest[pages_per_step:2 * pages_per_step]
    o_ref, qall_ref, qcm_ref, kpad_ref, vpad_ref, m_ref, l_ref, acc_ref = rest[2 * pages_per_step:]
    p = pl.program_id(1)
    rows, width = qcm_ref.shape

    @pl.when(p == 0)
    def _():
        m_ref[...] = jnp.full_like(m_ref, NEG)
        l_ref[...] = jnp.zeros_like(l_ref)
        acc_ref[...] = jnp.zeros_like(acc_ref)
        for h in range(N_ATT_HEADS):
            qall_ref[h * t_new:(h + 1) * t_new, :] = q_ref[:, h * ATT_HEAD_DIM:(h + 1) * ATT_HEAD_DIM]
        head_of_row = lax.broadcasted_iota(jnp.int32, (rows, width), 0) // t_new
        head_of_lane = lax.rem(lax.broadcasted_iota(jnp.int32, (rows, width), 1), N_ATT_HEADS)
        qcm_ref[...] = qc_ref[...] + jnp.where(head_of_row == head_of_lane, 0.0, NEG)

    qall = qall_ref[...].astype(BF16)

    def attend(pages):
        scores = [_dot_nt(qall, k_flat.astype(BF16)) * ATT_SCALE + bias for k_flat, _, bias in pages]
        m_cur = functools.reduce(jnp.maximum, [jnp.max(s, axis=-1, keepdims=True) for s in scores])
        m_prev = m_ref[...]
        m_new = jnp.maximum(m_prev, m_cur)
        alpha = jnp.exp(m_prev - m_new)
        l_new = alpha * l_ref[...]
        acc_new = alpha * acc_ref[...]
        for s, (_, v_flat, _) in zip(scores, pages):
            pm = jnp.exp(s - _tile_lanes(m_new, s.shape[1] // LANES))
            l_new = l_new + jnp.sum(pm, axis=-1, keepdims=True)
            acc_new = acc_new + _dot(pm.astype(BF16), v_flat.astype(BF16))
        l_ref[...] = l_new
        acc_ref[...] = acc_new
        m_ref[...] = m_new

    qcm = qcm_ref[...]
    attend([(k_refs[i][...].reshape(width, ATT_HEAD_DIM), v_refs[i][...].reshape(width, ATT_HEAD_DIM),
             qcm + pb_ref[i:i + 1, :]) for i in range(pages_per_step)])

    @pl.when(p == pl.num_programs(1) - 1)
    def _():
        n_new = kn_ref.shape[0]
        kpad_ref[...] = jnp.zeros_like(kpad_ref)
        vpad_ref[...] = jnp.zeros_like(vpad_ref)
        kpad_ref[0:n_new, :] = kn_ref[...]
        vpad_ref[0:n_new, :] = vn_ref[...]
        attend([(kpad_ref[...], vpad_ref[...], qc_ref[...] + bn_ref[...])])
        o_full = acc_ref[...] / l_ref[...]
        o = jnp.concatenate([o_full[h * t_new:(h + 1) * t_new, :] for h in range(N_ATT_HEADS)], axis=1)
        o_ref[...] = _rms(o, g_ref[...])


def _fox_sample(proj, k_new, v_new, qc_col, bias_new, pbias, cache_k, cache_v, page_table, g, layer,
                row0, t_new):
    n_seq, n_pages = page_table.shape
    page = cache_k.shape[2]
    width = page * N_ATT_HEADS
    pages_per_step = 16
    steps = n_pages // pages_per_step
    rows = N_ATT_HEADS * t_new
    n_new = t_new * N_ATT_HEADS
    rb0 = row0 // t_new

    def page_spec(i):
        return pl.BlockSpec(
            (None, None, page, N_ATT_HEADS, ATT_HEAD_DIM),
            lambda b, p, pt: (layer, pt[b * n_pages + p * pages_per_step + i], 0, 0, 0))

    per_seq = lambda shape: pl.BlockSpec((None,) + shape, lambda b, p, pt: (b, 0, 0))
    new_rows = lambda: per_seq((n_new, ATT_HEAD_DIM))
    grid_spec = pltpu.PrefetchScalarGridSpec(
        num_scalar_prefetch=1,
        grid=(n_seq, steps),
        in_specs=[
            pl.BlockSpec((t_new, ATT_WIDTH), lambda b, p, pt: (rb0 + b, COL_Q // ATT_WIDTH)),
            new_rows(), new_rows(),
            per_seq((rows, 1)), per_seq((rows, LANES)),
            pl.BlockSpec((None, None, pages_per_step, width), lambda b, p, pt: (layer, b, p, 0)),
            _layer_spec((1, ATT_WIDTH), layer),
        ] + [page_spec(i) for i in range(pages_per_step)] * 2,
        out_specs=pl.BlockSpec((t_new, ATT_WIDTH), lambda b, p, pt: (b, 0)),
        scratch_shapes=[
            pltpu.VMEM((rows, ATT_HEAD_DIM), F32),
            pltpu.VMEM((rows, width), F32),
            pltpu.VMEM((LANES, ATT_HEAD_DIM), F32),
            pltpu.VMEM((LANES, ATT_HEAD_DIM), F32),
            pltpu.VMEM((rows, LANES), F32),
            pltpu.VMEM((rows, LANES), F32),
            pltpu.VMEM((rows, ATT_HEAD_DIM), F32),
        ],
    )
    kern = functools.partial(_fox_sample_kernel, pages_per_step=pages_per_step, t_new=t_new)
    return pl.pallas_call(
        kern,
        grid_spec=grid_spec,
        out_shape=jax.ShapeDtypeStruct((n_seq * t_new, ATT_WIDTH), F32),
        compiler_params=_params("parallel", "arbitrary"),
        name="fox_sample",
    )(page_table.reshape(-1), proj, k_new, v_new, qc_col, bias_new, pbias, g,
      *([cache_k] * pages_per_step), *([cache_v] * pages_per_step))


def _causal_conv(x, prev, w, b, rows_per_seq):
    n = x.shape[0]
    row = lax.broadcasted_iota(jnp.int32, x.shape, 0)
    out = None
    for j in range(CONV_W - 1, 0, -1):
        if rows_per_seq is None:
            shifted = jnp.where(row >= j, pltpu.roll(x, j, 0), pltpu.roll(prev, j, 0))
        else:
            shifted = jnp.where(lax.rem(row, rows_per_seq) >= j, pltpu.roll(x, j, 0),
                                pltpu.roll(prev, (j - rows_per_seq) % n, 0))
        tap = CONV_W - 1 - j
        term = shifted * w[tap:tap + 1, :]
        out = term if out is None else out + term
    out = out + x * w[CONV_W - 1:CONV_W, :]
    out = out + b
    return out * _sigmoid(out)


def _ssd_scalars(misc, dtb, alog, valid, tri, lsel):
    dtf = _softplus(misc + dtb)
    a = dtf * (-(jnp.exp(alog) * valid))
    acum = _select_rows(tri, a)
    if lsel is None:
        acum_end = acum[CHUNK - 1:CHUNK, :]
    else:
        acum_end = _select_rows(lsel, acum)
    return dtf, acum, acum_end


def _ssd_diag(xs, bm, cm, dtf, acum, visible):
    acum_t = acum.T
    dt_t = dtf.T
    lane = lax.broadcasted_iota(jnp.int32, (xs.shape[0], LANES), 1)
    heads_per_group = N_SSM_HEADS // N_SSM_GROUPS
    blocks = []
    for g in range(N_SSM_GROUPS):
        gs = slice(g * D_STATE, (g + 1) * D_STATE)
        cb = _dot_nt(cm[:, gs].astype(BF16), bm[:, gs].astype(BF16))
        for j in range(heads_per_group // 2):
            blk = g * (heads_per_group // 2) + j
            xpair = xs[:, blk * LANES:(blk + 1) * LANES]
            y = None
            for hh in range(2):
                ch = DT_LANE + 2 * blk + hh
                seg = acum[:, ch:ch + 1] - acum_t[ch:ch + 1, :]
                decay = jnp.exp(jnp.where(visible, seg, NEG))
                m = (cb * decay) * dt_t[ch:ch + 1, :]
                in_half = (lane >= SSM_HEADDIM) if hh else (lane < SSM_HEADDIM)
                xh = jnp.where(in_half, xpair, 0.0)
                part = _dot(m.astype(BF16), xh.astype(BF16))
                y = part if y is None else y + part
            blocks.append(y)
    return jnp.concatenate(blocks, axis=1)


def _ssd_prompt_kernel(z_ref, xs_ref, bc_ref, misc_ref, cw_ref, cb_ref, dtb_ref, alog_ref,
                       valid_ref, drow_ref, g_ref, fb_ref, tri_ref, e_ref,
                       o_ref, st_ref, logf_ref, cum_ref, cumt_ref,
                       ht_ref, pxs_ref, pbc_ref, fsum_ref, *, chunks_per_seq):
    c = lax.rem(pl.program_id(0), chunks_per_seq)

    @pl.when(c == 0)
    def _():
        ht_ref[...] = jnp.zeros_like(ht_ref)
        pxs_ref[...] = jnp.zeros_like(pxs_ref)
        pbc_ref[...] = jnp.zeros_like(pbc_ref)
        fsum_ref[...] = jnp.zeros_like(fsum_ref)

    logf, fsum = _forget_gates(misc_ref[...], fb_ref[...], tri_ref[...], fsum_ref[...])
    fsum_ref[...] = fsum[CHUNK - 1:CHUNK, :]
    logf_ref[...] = _head_lanes(logf)
    cum_ref[...] = _head_lanes(fsum)
    cumt_ref[...] = fsum.T[F_LANE:F_LANE + N_ATT_HEADS, :]

    xs_raw = xs_ref[...]
    bc_raw = bc_ref[...]
    cw = cw_ref[...]
    cb = cb_ref[...]
    xs = _causal_conv(xs_raw, pxs_ref[...], cw[:, :SSM_WIDTH], cb[:, :SSM_WIDTH], None)
    bc = _causal_conv(bc_raw, pbc_ref[...], cw[:, SSM_WIDTH:], cb[:, SSM_WIDTH:], None)
    pxs_ref[...] = xs_raw
    pbc_ref[...] = bc_raw
    bm = bc[:, :N_SSM_GROUPS * D_STATE]
    cm = bc[:, N_SSM_GROUPS * D_STATE:]

    dtf, acum, acum_end = _ssd_scalars(misc_ref[...], dtb_ref[...], alog_ref[...], valid_ref[...],
                                       tri_ref[...], None)
    row = lax.broadcasted_iota(jnp.int32, (CHUNK, CHUNK), 0)
    col = lax.broadcasted_iota(jnp.int32, (CHUNK, CHUNK), 1)
    y = _ssd_diag(xs, bm, cm, dtf, acum, col <= row)

    e = e_ref[...]
    exp_a = jnp.exp(_select_cols(acum, e))
    w_end = _select_cols(jnp.exp(acum_end - acum) * dtf, e)
    xw = xs * w_end
    ht = ht_ref[...]
    y_off = []
    new_states = []
    for g in range(N_SSM_GROUPS):
        gs = slice(g * D_STATE, (g + 1) * D_STATE)
        ws = slice(g * GROUP_WIDTH, (g + 1) * GROUP_WIDTH)
        y_off.append(_dot(cm[:, gs].astype(BF16), ht[:, ws].astype(BF16)))
        new_states.append(_dot(bm[:, gs].T.astype(BF16), xw[:, ws].astype(BF16)))
    y = y + jnp.concatenate(y_off, axis=1) * exp_a
    ht_new = exp_a[CHUNK - 1:CHUNK, :] * ht + jnp.concatenate(new_states, axis=1)
    ht_ref[...] = ht_new

    y = y + drow_ref[...] * xs
    z = z_ref[...]
    y = y * (z * _sigmoid(z))
    o_ref[...] = _rms(y, g_ref[...])

    @pl.when(c == chunks_per_seq - 1)
    def _():
        for j in range(SSM_WIDTH // LANES):
            st_ref[j * LANES:(j + 1) * LANES, :] = ht_new[:, j * LANES:(j + 1) * LANES].T


def _ssd_consts(l_rows_per_seq):
    r = lax.broadcasted_iota(jnp.int32, (CHUNK, CHUNK), 0)
    c = lax.broadcasted_iota(jnp.int32, (CHUNK, CHUNK), 1)
    same = (r // l_rows_per_seq) == (c // l_rows_per_seq)
    tri = (same & (c <= r)).astype(BF16)
    lsel = (c == (r // l_rows_per_seq) * l_rows_per_seq + l_rows_per_seq - 1).astype(BF16)
    return tri, lsel


def _expand_matrix():
    r = lax.broadcasted_iota(jnp.int32, (LANES, SSM_WIDTH), 0)
    c = lax.broadcasted_iota(jnp.int32, (LANES, SSM_WIDTH), 1)
    return (r == DT_LANE + c // SSM_HEADDIM).astype(BF16)


def _row_spec(shape):
    return pl.BlockSpec(shape, lambda i: (0, 0))


def _ssd_param_specs(layer):
    return [_layer_spec((CONV_W, CONV_DIM), layer), _layer_spec((1, CONV_DIM), layer),
            _layer_spec((1, LANES), layer), _layer_spec((1, LANES), layer), _row_spec((1, LANES)),
            _layer_spec((1, SSM_WIDTH), layer), _layer_spec((1, SSM_WIDTH), layer),
            _layer_spec((1, LANES), layer)]


def _ssd_prompt(proj, cw, cb, dtb, alog, valid, drow, g, fb, e, layer, batch, seq):
    chunks_per_seq = seq // CHUNK
    tri, _ = _ssd_consts(CHUNK)
    kern = functools.partial(_ssd_prompt_kernel, chunks_per_seq=chunks_per_seq)
    return pl.pallas_call(
        kern,
        grid=(batch * chunks_per_seq,),
        in_specs=[
            pl.BlockSpec((CHUNK, SSM_WIDTH), lambda i: (i, COL_Z // SSM_WIDTH)),
            pl.BlockSpec((CHUNK, SSM_WIDTH), lambda i: (i, COL_XS // SSM_WIDTH)),
            pl.BlockSpec((CHUNK, BC_WIDTH), lambda i: (i, COL_BC // BC_WIDTH)),
            pl.BlockSpec((CHUNK, LANES), lambda i: (i, COL_MISC // LANES)),
            *_ssd_param_specs(layer),
            _row_spec((CHUNK, CHUNK)), _row_spec((LANES, SSM_WIDTH)),
        ],
        out_specs=[
            pl.BlockSpec((CHUNK, SSM_WIDTH), lambda i: (i, 0)),
            pl.BlockSpec((None, SSM_WIDTH, D_STATE), lambda i: (i // chunks_per_seq, 0, 0)),
            pl.BlockSpec((CHUNK, N_ATT_HEADS), lambda i: (i, 0)),
            pl.BlockSpec((CHUNK, N_ATT_HEADS), lambda i: (i, 0)),
            pl.BlockSpec((N_ATT_HEADS, CHUNK), lambda i: (0, i)),
        ],
        out_shape=[
            jax.ShapeDtypeStruct((batch * seq, SSM_WIDTH), F32),
            jax.ShapeDtypeStruct((batch, SSM_WIDTH, D_STATE), F32),
            jax.ShapeDtypeStruct((batch * seq, N_ATT_HEADS), F32),
            jax.ShapeDtypeStruct((batch * seq, N_ATT_HEADS), F32),
            jax.ShapeDtypeStruct((N_ATT_HEADS, batch * seq), F32),
        ],
        scratch_shapes=[
            pltpu.VMEM((D_STATE, SSM_WIDTH), F32),
            pltpu.VMEM((CHUNK, SSM_WIDTH), F32),
            pltpu.VMEM((CHUNK, BC_WIDTH), F32),
            pltpu.VMEM((1, LANES), F32),
        ],
        compiler_params=_params("arbitrary"),
        name="ssd_prompt",
    )(proj, proj, proj, proj, cw, cb, dtb, alog, valid, drow, g, fb, tri, e)


def _ssd_sample_kernel(z_ref, xs_ref, bc_ref, misc_ref, prev_ref, h0_ref, cw_ref, cb_ref, dtb_ref,
                       alog_ref, valid_ref, drow_ref, g_ref, fb_ref, tri_ref, lsel_ref, e_ref,
                       o_ref, st_ref, logf_ref, cumt_ref, *, n_seq, t_new):
    n_rows = n_seq * t_new
    live = lax.broadcasted_iota(jnp.int32, (CHUNK, 1), 0) < n_rows
    xs_raw = jnp.where(live, xs_ref[...], 0.0)
    bc_raw = jnp.where(live, bc_ref[...], 0.0)
    z = jnp.where(live, z_ref[...], 0.0)
    misc = jnp.where(live, misc_ref[...], 0.0)
    logf, fsum = _forget_gates(misc, fb_ref[...], tri_ref[...], 0.0)
    logf_ref[...] = _head_lanes(logf)[0:n_rows, :]
    cumt_ref[...] = fsum.T[F_LANE:F_LANE + N_ATT_HEADS, :]
    cw = cw_ref[...]
    cb = cb_ref[...]
    prev = prev_ref[...]
    xs = _causal_conv(xs_raw, prev[:, :SSM_WIDTH], cw[:, :SSM_WIDTH], cb[:, :SSM_WIDTH], t_new)
    bc = _causal_conv(bc_raw, prev[:, SSM_WIDTH:], cw[:, SSM_WIDTH:], cb[:, SSM_WIDTH:], t_new)
    bm = bc[:, :N_SSM_GROUPS * D_STATE]
    cm = bc[:, N_SSM_GROUPS * D_STATE:]

    dtf, acum, acum_end = _ssd_scalars(misc, dtb_ref[...], alog_ref[...], valid_ref[...],
                                       tri_ref[...], lsel_ref[...])
    row = lax.broadcasted_iota(jnp.int32, (CHUNK, CHUNK), 0)
    col = lax.broadcasted_iota(jnp.int32, (CHUNK, CHUNK), 1)
    visible = (row // t_new == col // t_new) & (col <= row)
    y = _ssd_diag(xs, bm, cm, dtf, acum, visible)

    e = e_ref[...]
    exp_a = jnp.exp(_select_cols(acum, e))
    w_end = _select_cols(jnp.exp(acum_end - acum) * dtf, e)
    xw = xs * w_end
    bm_t = [bm[:, g * D_STATE:(g + 1) * D_STATE].T for g in range(N_SSM_GROUPS)]
    lane_seq = lax.broadcasted_iota(jnp.int32, (D_STATE, CHUNK), 1) // t_new

    y_off_rows = []
    for b in range(n_seq):
        rows = slice(b * t_new, (b + 1) * t_new)
        h0 = h0_ref[b]
        ht = jnp.concatenate(
            [h0[j * LANES:(j + 1) * LANES, :].T for j in range(SSM_WIDTH // LANES)], axis=1)
        y_off = []
        new_states = []
        for g in range(N_SSM_GROUPS):
            gs = slice(g * D_STATE, (g + 1) * D_STATE)
            ws = slice(g * GROUP_WIDTH, (g + 1) * GROUP_WIDTH)
            y_off.append(_dot(cm[:, gs].astype(BF16), ht[:, ws].astype(BF16))[rows, :])
            bt = jnp.where(lane_seq == b, bm_t[g], 0.0)
            new_states.append(_dot(bt.astype(BF16), xw[:, ws].astype(BF16)))
        y_off_rows.append(jnp.concatenate(y_off, axis=1))
        last = (b + 1) * t_new - 1
        ht_new = exp_a[last:last + 1, :] * ht + jnp.concatenate(new_states, axis=1)
        for j in range(SSM_WIDTH // LANES):
            st_ref[b, j * LANES:(j + 1) * LANES, :] = ht_new[:, j * LANES:(j + 1) * LANES].T
    y_off_rows.append(jnp.zeros((CHUNK - n_rows, SSM_WIDTH), F32))
    y = y + jnp.concatenate(y_off_rows, axis=0) * exp_a

    y = y + drow_ref[...] * xs
    y = y * (z * _sigmoid(z))
    o_ref[...] = _rms(y, g_ref[...])[0:n_rows, :]


def _ssd_sample(proj, prev, h0, cw, cb, dtb, alog, valid, drow, g, fb, e, layer, row0, n_seq, t_new):
    tri, lsel = _ssd_consts(t_new)
    rb = row0 // CHUNK
    kern = functools.partial(_ssd_sample_kernel, n_seq=n_seq, t_new=t_new)
    full3 = pl.BlockSpec((n_seq, SSM_WIDTH, D_STATE), lambda i: (0, 0, 0))
    return pl.pallas_call(
        kern,
        grid=(1,),
        in_specs=[
            pl.BlockSpec((CHUNK, SSM_WIDTH), lambda i: (rb, COL_Z // SSM_WIDTH)),
            pl.BlockSpec((CHUNK, SSM_WIDTH), lambda i: (rb, COL_XS // SSM_WIDTH)),
            pl.BlockSpec((CHUNK, BC_WIDTH), lambda i: (rb, COL_BC // BC_WIDTH)),
            pl.BlockSpec((CHUNK, LANES), lambda i: (rb, COL_MISC // LANES)),
            _layer_spec((CHUNK, CONV_DIM), layer),
            _layer_spec((n_seq, SSM_WIDTH, D_STATE), layer),
            *_ssd_param_specs(layer),
            _row_spec((CHUNK, CHUNK)), _row_spec((CHUNK, CHUNK)), _row_spec((LANES, SSM_WIDTH)),
        ],
        out_specs=[
            pl.BlockSpec((n_seq * t_new, SSM_WIDTH), lambda i: (0, 0)),
            full3,
            pl.BlockSpec((n_seq * t_new, N_ATT_HEADS), lambda i: (0, 0)),
            pl.BlockSpec((N_ATT_HEADS, CHUNK), lambda i: (0, 0)),
        ],
        out_shape=[
            jax.ShapeDtypeStruct((n_seq * t_new, SSM_WIDTH), F32),
            jax.ShapeDtypeStruct((n_seq, SSM_WIDTH, D_STATE), F32),
            jax.ShapeDtypeStruct((n_seq * t_new, N_ATT_HEADS), F32),
            jax.ShapeDtypeStruct((N_ATT_HEADS, CHUNK), F32),
        ],
        compiler_params=_params("arbitrary"),
        name="ssd_sample",
    )(proj, proj, proj, proj, prev, h0, cw, cb, dtb, alog, valid, drow, g, fb, tri, lsel, e)


def _out_proj_kernel(h_ref, ap_ref, sp_ref, as_ref, ss_ref, wa_ref, ws_ref, o_ref, *, n_prompt_tiles):
    i = pl.program_id(0)

    def mix(att, ssm):
        o_ref[...] = h_ref[...] + (_dot(att.astype(BF16), wa_ref[...]) + _dot(ssm.astype(BF16), ws_ref[...]))

    @pl.when(i < n_prompt_tiles)
    def _():
        mix(ap_ref[...], sp_ref[...])

    @pl.when(i >= n_prompt_tiles)
    def _():
        pad = jnp.zeros((h_ref.shape[0] - as_ref.shape[0], as_ref.shape[1]), F32)
        mix(jnp.concatenate([as_ref[...], pad], axis=0), jnp.concatenate([ss_ref[...], pad], axis=0))


def _out_proj(h, o_att_p, o_ssm_p, o_att_s, o_ssm_s, w, layer):
    t = h.shape[0]
    tm = ROW_ALIGN
    n_prompt, n_sample = o_att_p.shape[0], o_att_s.shape[0]
    n_prompt_tiles = n_prompt // tm
    assert n_prompt % tm == 0 and t // tm == n_prompt_tiles + 1 and n_sample <= tm
    prompt_spec = pl.BlockSpec((tm, ATT_WIDTH), lambda i: (jnp.minimum(i, n_prompt_tiles - 1), 0))
    sample_spec = pl.BlockSpec((n_sample, ATT_WIDTH), lambda i: (0, 0))
    return pl.pallas_call(
        functools.partial(_out_proj_kernel, n_prompt_tiles=n_prompt_tiles),
        grid=(t // tm,),
        in_specs=[
            pl.BlockSpec((tm, D_MODEL), lambda i: (i, 0)),
            prompt_spec, prompt_spec, sample_spec, sample_spec,
            pl.BlockSpec((None, ATT_WIDTH, D_MODEL), lambda i: (layer, 0, 0)),
            pl.BlockSpec((None, SSM_WIDTH, D_MODEL), lambda i: (layer, 1, 0)),
        ],
        out_specs=pl.BlockSpec((tm, D_MODEL), lambda i: (i, 0)),
        out_shape=jax.ShapeDtypeStruct((t, D_MODEL), F32),
        compiler_params=_params("arbitrary"),
        name="out_proj",
    )(h, o_att_p, o_ssm_p, o_att_s, o_ssm_s, w, w)


def _mlp_kernel(h_ref, g_ref, wu_ref, wd_ref, o_ref, hn_ref):
    @pl.when(pl.program_id(1) == 0)
    def _():
        h = h_ref[...]
        hn_ref[...] = _rms(h, g_ref[...]).astype(BF16)
        o_ref[...] = h

    u = jnp.maximum(_dot(hn_ref[...], wu_ref[...]), 0.0)
    o_ref[...] += _dot((u * u).astype(BF16), wd_ref[...])


def _mlp(h, g, wu, wd, layer):
    t = h.shape[0]
    tm = TOKEN_TILE
    tf = 1024
    return pl.pallas_call(
        _mlp_kernel,
        grid=(t // tm, D_FF // tf),
        in_specs=[
            pl.BlockSpec((tm, D_MODEL), lambda i, f: (i, 0)),
            _layer_spec((1, D_MODEL), layer),
            pl.BlockSpec((None, D_MODEL, tf), lambda i, f: (layer, 0, f)),
            pl.BlockSpec((None, tf, D_MODEL), lambda i, f: (layer, f, 0)),
        ],
        out_specs=pl.BlockSpec((tm, D_MODEL), lambda i, f: (i, 0)),
        out_shape=jax.ShapeDtypeStruct((t, D_MODEL), F32),
        scratch_shapes=[pltpu.VMEM((tm, D_MODEL), BF16)],
        compiler_params=_params("parallel", "arbitrary"),
        name="mlp",
    )(h, g, wu, wd)


def _ple_update(h_ref, p_ref, g_ref, wg_ref, wp_ref):
    h = h_ref[...]
    gate = _sigmoid(_dot(_rms(h, g_ref[...]).astype(BF16), wg_ref[...]))
    return h + _dot(p_ref[...].astype(BF16), wp_ref[...]) * gate


def _ple_kernel(h_ref, p_ref, g_ref, wg_ref, wp_ref, o_ref):
    o_ref[...] = _ple_update(h_ref, p_ref, g_ref, wg_ref, wp_ref)


def _ple_final_kernel(h_ref, p_ref, g_ref, wg_ref, wp_ref, gf_ref, yp_ref, ys_ref, *, n_prompt_tiles):
    y = _rms(_ple_update(h_ref, p_ref, g_ref, wg_ref, wp_ref), gf_ref[...])
    i = pl.program_id(0)

    @pl.when(i < n_prompt_tiles)
    def _():
        yp_ref[...] = y

    @pl.when(i >= n_prompt_tiles)
    def _():
        ys_ref[...] = y[0:ys_ref.shape[0], :]


def _ple(h, p, g, wg, wp, layer, g_final=None, n_prompt=None, n_sample=None):
    t = h.shape[0]
    tm = ROW_ALIGN
    ple_dim = p.shape[-1]
    row = lambda: pl.BlockSpec((1, D_MODEL), lambda i: (0, 0))
    in_specs = [
        pl.BlockSpec((tm, D_MODEL), lambda i: (i, 0)),
        pl.BlockSpec((None, tm, ple_dim), lambda i: (layer, i, 0)),
        _layer_spec((1, D_MODEL), layer),
        pl.BlockSpec((None, D_MODEL, D_MODEL), lambda i: (layer, 0, 0)),
        pl.BlockSpec((None, ple_dim, D_MODEL), lambda i: (layer, 0, 0)),
    ]
    if g_final is None:
        return pl.pallas_call(
            _ple_kernel,
            grid=(t // tm,),
            in_specs=in_specs,
            out_specs=pl.BlockSpec((tm, D_MODEL), lambda i: (i, 0)),
            out_shape=jax.ShapeDtypeStruct((t, D_MODEL), F32),
            compiler_params=_params("parallel"),
            name="ple",
        )(h, p, g, wg, wp)
    n_prompt_tiles = n_prompt // tm
    assert n_prompt % tm == 0 and t // tm == n_prompt_tiles + 1 and n_sample <= tm
    return pl.pallas_call(
        functools.partial(_ple_final_kernel, n_prompt_tiles=n_prompt_tiles),
        grid=(t // tm,),
        in_specs=in_specs + [row()],
        out_specs=[
            pl.BlockSpec((tm, D_MODEL), lambda i: (jnp.minimum(i, n_prompt_tiles - 1), 0)),
            pl.BlockSpec((n_sample, D_MODEL), lambda i: (0, 0)),
        ],
        out_shape=[jax.ShapeDtypeStruct((n_prompt, D_MODEL), F32),
                   jax.ShapeDtypeStruct((n_sample, D_MODEL), F32)],
        compiler_params=_params("arbitrary"),
        name="ple_final",
    )(h, p, g, wg, wp, g_final)


def _stack_rows_kernel(*refs, n_layers):
    k_refs, v_refs = refs[:n_layers], refs[n_layers:2 * n_layers]
    ko_ref, vo_ref = refs[2 * n_layers:]
    layer = pl.program_id(0)
    for n in range(n_layers):
        @pl.when(layer == n)
        def _():
            ko_ref[...] = k_refs[n][...]
            vo_ref[...] = v_refs[n][...]


def _stack_rows(k_layers, v_layers, n_rows):
    n_layers = len(k_layers)
    rows = 2 * ROW_ALIGN
    assert n_rows % rows == 0
    n_tiles = n_rows // rows
    tail = k_layers[0].shape[1:]
    zeros = (0,) * len(tail)
    out = jax.ShapeDtypeStruct((n_layers, n_rows) + tail, k_layers[0].dtype)

    def src_spec(n):
        return pl.BlockSpec((rows,) + tail,
                            lambda l, i: (jnp.where(l == n, i, jnp.where(l > n, n_tiles - 1, 0)),) + zeros)

    out_spec = pl.BlockSpec((None, rows) + tail, lambda l, i: (l, i) + zeros)
    return pl.pallas_call(
        functools.partial(_stack_rows_kernel, n_layers=n_layers),
        grid=(n_layers, n_tiles),
        in_specs=[src_spec(n) for n in range(n_layers)] * 2,
        out_specs=[out_spec, out_spec],
        out_shape=[out, out],
        compiler_params=_params("arbitrary", "arbitrary"),
        name="stack_kv",
    )(*k_layers, *v_layers)


def _misc_rows(values, lane0):
    n = values.shape[1]
    return jnp.pad(values.astype(F32), ((0, 0), (lane0, LANES - lane0 - n)))[:, None, :]


def _rows(x):
    return x.astype(F32)[:, None, :]


def kernel(x_prompt, x_sample, cache_k, cache_v, cache_logf, state_ssm, state_conv, page_table, p_prompt, p_sample, norm_mix, w_in, b_forget, conv_w, conv_b, dt_bias, a_log, d_skip, norm_attn_out, norm_ssm_out, w_out, norm_mlp, w_up, w_down, norm_ple, w_ple_gate, w_ple_proj, norm_final):
    batch, seq, _ = x_prompt.shape
    n_seq, t_new, _ = x_sample.shape
    depth = w_in.shape[0]
    n_prompt = batch * seq
    n_sample = n_seq * t_new
    t_pad = -(-(n_prompt + CHUNK) // TOKEN_TILE) * TOKEN_TILE
    n_tail = t_pad - n_prompt - n_sample
    n_phys, page = cache_k.shape[1], cache_k.shape[2]

    h = jnp.concatenate([x_prompt.reshape(n_prompt, D_MODEL), x_sample.reshape(n_sample, D_MODEL),
                         jnp.zeros((n_tail, D_MODEL), F32)], axis=0)
    ple_dim = p_prompt.shape[-1]
    p_all = jnp.concatenate([p_prompt.reshape(depth, n_prompt, ple_dim),
                             p_sample.reshape(depth, n_sample, ple_dim),
                             jnp.zeros((depth, n_tail, ple_dim), F32)], axis=1)
    w_in_r = _reorder_w_in(w_in)
    w_out_b = w_out.astype(BF16)
    w_up_b = w_up.astype(BF16)
    w_down_b = w_down.astype(BF16)
    w_gate_b = w_ple_gate.astype(BF16)
    w_proj_b = w_ple_proj.astype(BF16)

    expand = _expand_matrix()
    valid_row = _misc_rows(jnp.ones((1, N_SSM_HEADS), F32), DT_LANE)[0]
    ssm_params = (conv_w, _rows(conv_b), _misc_rows(dt_bias, DT_LANE), _misc_rows(a_log, DT_LANE),
                  valid_row, _rows(jnp.repeat(d_skip, SSM_HEADDIM, axis=1)), _rows(norm_ssm_out),
                  _misc_rows(b_forget, F_LANE), expand)
    g_mix, g_att, g_mlp, g_ple = _rows(norm_mix), _rows(norm_attn_out), _rows(norm_mlp), _rows(norm_ple)
    prev = jnp.pad(state_conv, ((0, 0), (0, 0), (t_new - (CONV_W - 1), 0), (0, 0)))
    prev = jnp.pad(prev.reshape(depth, n_sample, CONV_DIM), ((0, 0), (0, CHUNK - n_sample), (0, 0)))
    h0 = state_ssm.reshape(depth, n_seq, SSM_WIDTH, D_STATE)
    pbias = _past_bias(cache_logf, page_table)
    causal_new = jnp.arange(t_new)[None, :] <= jnp.arange(t_new)[:, None]
    same_head = jnp.eye(N_ATT_HEADS, dtype=bool)

    outs = {name: [] for name in ("kp", "vp", "fp", "sp", "cp", "ks", "vs", "fs", "ss", "cs")}
    n_new = t_new * N_ATT_HEADS
    for l in range(depth):
        proj, kv, k_rows, v_rows, k_new, v_new = _in_proj(h, g_mix, w_in_r, l, n_prompt, n_sample)
        o_ssm_p, st_p, logf_p, cum, cumt = _ssd_prompt(proj, *ssm_params, l, batch, seq)
        o_ssm_s, st_s, logf_s, cumt_s = _ssd_sample(proj, prev, h0, *ssm_params, l, n_prompt, n_seq, t_new)

        o_att_p = _fox_prompt(proj, kv, cum, cumt, g_att, l, batch, seq)
        qc = cumt_s[:, :n_sample].reshape(N_ATT_HEADS, n_seq, t_new)
        qc = jnp.transpose(qc, (1, 0, 2))
        qc_col = qc.reshape(n_seq, N_ATT_HEADS * t_new, 1)
        bias_new = jnp.where(causal_new[None, None, :, :, None] & same_head[None, :, None, None, :],
                             -qc[:, :, None, :, None], NEG)
        bias_new = jnp.concatenate(
            [bias_new.reshape(n_seq, N_ATT_HEADS * t_new, t_new * N_ATT_HEADS),
             jnp.full((n_seq, N_ATT_HEADS * t_new, LANES - t_new * N_ATT_HEADS), NEG, F32)], axis=2)
        o_att_s = _fox_sample(proj, k_new.reshape(n_seq, n_new, ATT_HEAD_DIM),
                              v_new.reshape(n_seq, n_new, ATT_HEAD_DIM), qc_col, bias_new,
                              pbias, cache_k, cache_v, page_table, g_att, l, n_prompt, t_new)

        h = _out_proj(h, o_att_p, o_ssm_p, o_att_s, o_ssm_s, w_out_b, l)
        h = _mlp(h, g_mlp, w_up_b, w_down_b, l)
        ple_args = (h, p_all, g_ple, w_gate_b, w_proj_b, l)
        if l < depth - 1:
            h = _ple(*ple_args)
        else:
            y_prompt, y_sample = _ple(*ple_args, norm_final[None, :], n_prompt, n_sample)

        xbc = slice(COL_XS, COL_XS + CONV_DIM)
        outs["kp"].append(k_rows)
        outs["vp"].append(v_rows)
        outs["ks"].append(k_new)
        outs["vs"].append(v_new)
        outs["fp"].append(logf_p.reshape(batch, seq, N_ATT_HEADS))
        outs["sp"].append(st_p.reshape(batch, N_SSM_HEADS, SSM_HEADDIM, D_STATE))
        outs["cp"].append(jnp.stack(
            [proj[(b + 1) * seq - (CONV_W - 1):(b + 1) * seq, xbc] for b in range(batch)]))
        outs["fs"].append(logf_s.reshape(n_seq, t_new, N_ATT_HEADS))
        outs["ss"].append(st_s.reshape(n_seq, N_SSM_HEADS, SSM_HEADDIM, D_STATE))
        outs["cs"].append(proj[n_prompt:n_prompt + n_sample].reshape(n_seq, t_new, PROJ_OUT_W)
                          [:, t_new - (CONV_W - 1):, xbc])

    kv_shape_p = (depth, batch, seq, N_ATT_HEADS, ATT_HEAD_DIM)
    kv_shape_s = (depth, n_seq, t_new, N_ATT_HEADS, ATT_HEAD_DIM)
    stack = lambda name: jnp.stack(outs[name])
    k_prompt, v_prompt = _stack_rows(outs["kp"], outs["vp"], n_prompt)
    return (y_prompt.reshape(batch, seq, D_MODEL), y_sample.reshape(n_seq, t_new, D_MODEL),
            k_prompt.reshape(kv_shape_p), v_prompt.reshape(kv_shape_p),
            stack("fp"), stack("sp"), stack("cp"),
            stack("ks").reshape(kv_shape_s), stack("vs").reshape(kv_shape_s),
            stack("fs"), stack("ss"), stack("cs"))
```
